```python
import jax
import jax.numpy as jnp
from jax import lax
import numpy as np

D_MODEL = 4096
BATCH = 8
SEQ = 2048
DEPTH = 4

CHUNK = 64
Q_BLOCK = 128
NORM_EPS = 1e-6

A_HEADS = 16
A_HEAD_DIM = 64
A_WIDTH = A_HEADS * A_HEAD_DIM
A_DECAY_LORA = 64
A_ICLR_LORA = 64
A_GATE_LORA = 160
A_GN_EPS = 64e-5
A_SHIFT_COLS = 3 * A_WIDTH + A_DECAY_LORA + A_ICLR_LORA + A_GATE_LORA
A_SPLITS = (A_WIDTH, 2 * A_WIDTH, 3 * A_WIDTH, 3 * A_WIDTH + A_DECAY_LORA,
            3 * A_WIDTH + A_DECAY_LORA + A_ICLR_LORA)

B_HEADS = 12
B_HEAD_DIM = 128
B_WIDTH = B_HEADS * B_HEAD_DIM
B_COLS = 3 * B_WIDTH + B_HEADS

C_HEADS = 12
C_HEAD_DIM = 128
C_WIDTH = C_HEADS * C_HEAD_DIM
C_COLS = 3 * C_WIDTH

N_BRANCH = 3
GATE_COLS = N_BRANCH * D_MODEL
IN_COLS = A_SHIFT_COLS + B_COLS + C_COLS + GATE_COLS
MIX_SPLITS = (A_SHIFT_COLS, A_SHIFT_COLS + B_COLS, A_SHIFT_COLS + B_COLS + C_COLS)

D_FF = 5120
CONV_WIDTH = 3
N_MOD = 6

kernel_name = "hybrid_rwkv7_fox_stickbreaking_encoder"


def rms_norm(x, gain):
    x32 = x.astype(jnp.float32)
    y = x32 * lax.rsqrt(jnp.mean(x32 * x32, axis=-1, keepdims=True) + NORM_EPS)
    return (y * gain.astype(jnp.float32)).astype(x.dtype)


def to_heads(t, n_heads, head_dim):
    b, s, _ = t.shape
    return t.reshape(b, s, n_heads, head_dim).transpose(0, 2, 1, 3)


def from_heads(t):
    b, h, s, d = t.shape
    return t.transpose(0, 2, 1, 3).reshape(b, s, h * d)


def rwkv7_recurrence(r, decay, k, v, kk, iclr):
    b, s, h, n = r.shape
    n_chunks = s // CHUNK

    def chunks(t):
        return t.astype(jnp.float32).reshape(b, n_chunks, CHUNK, h, n).transpose(1, 2, 0, 3, 4)

    inputs = (chunks(r), chunks(decay), chunks(k), chunks(v), chunks(-kk), chunks(kk * iclr))

    def step(state, inp):
        r_t, w_t, k_t, v_t, kneg_t, kb_t = inp
        sa = jnp.einsum('bhij,bhj->bhi', state, kneg_t)
        state = (state * w_t[:, :, None, :] + sa[..., None] * kb_t[:, :, None, :]
                 + v_t[..., None] * k_t[:, :, None, :])
        return state, jnp.einsum('bhij,bhj->bhi', state, r_t)

    def chunk_step(state, chunk_inp):
        return lax.scan(step, state, chunk_inp)

    state0 = jnp.zeros((b, h, n, n), jnp.float32)
    _, y = lax.scan(chunk_step, state0, inputs)
    return y.transpose(2, 0, 1, 3, 4).reshape(b, s, h, n)


def rwkv7_time_mix(pa, mu, w0, w2, a0, a2, g2, kk_scale, k_a, r_k, gn_w, gn_b):
    b, s, _ = pa.shape
    prev = jnp.pad(pa, ((0, 0), (1, 0), (0, 0)))[:, :-1]
    z = pa + (prev - pa) * mu
    r, k, v, zw, za, zg = jnp.split(z, A_SPLITS, axis=-1)
    log_w = -jax.nn.softplus(-(w0 + jnp.tanh(zw) @ w2)) - 0.5
    decay = jnp.exp(-jnp.exp(log_w.astype(jnp.float32)))
    iclr = jax.nn.sigmoid(a0 + za @ a2)
    gate = jax.nn.sigmoid(zg) @ g2

    def heads(t):
        return t.reshape(b, s, A_HEADS, A_HEAD_DIM)

    r, k, v, decay, iclr = heads(r), heads(k), heads(v), heads(decay), heads(iclr)
    kk = (k * kk_scale).astype(jnp.float32)
    kk = kk / jnp.maximum(jnp.sqrt(jnp.sum(kk * kk, axis=-1, keepdims=True)), 1e-12)
    k = k * (1 + (iclr - 1) * k_a)
    y = rwkv7_recurrence(r, decay, k, v, kk, iclr)
    mean = jnp.mean(y, axis=-1, keepdims=True)
    var = jnp.mean(jnp.square(y - mean), axis=-1, keepdims=True)
    y = ((y - mean) * lax.rsqrt(var + A_GN_EPS)).reshape(b, s, A_WIDTH) * gn_w + gn_b
    bonus = (jnp.sum(r * k * r_k, axis=-1, keepdims=True) * v).reshape(b, s, A_WIDTH)
    return ((y + bonus.astype(jnp.float32)) * gate.astype(jnp.float32)).astype(pa.dtype)


def fox_attention(q, k, v, f_logit):
    _, _, s_len, dh = q.shape
    scale = dh ** -0.5
    cum = jnp.cumsum(jax.nn.log_sigmoid(f_logit.astype(jnp.float32)), axis=1).transpose(0, 2, 1)
    outs = []
    for i in range(s_len // Q_BLOCK):
        q0, q1 = i * Q_BLOCK, (i + 1) * Q_BLOCK
        logits = jnp.einsum('bhqd,bhkd->bhqk', q[:, :, q0:q1], k[:, :, :q1]).astype(jnp.float32) * scale
        logits = logits + cum[:, :, q0:q1, None] - cum[:, :, None, :q1]
        causal = jnp.arange(q0, q1)[:, None] >= jnp.arange(q1)[None, :]
        p = jax.nn.softmax(jnp.where(causal, logits, -jnp.inf), axis=-1)
        outs.append(jnp.einsum('bhqk,bhkd->bhqd', p.astype(v.dtype), v[:, :, :q1]))
    return jnp.concatenate(outs, axis=2)


def stick_breaking_attention(q, k, v):
    _, _, s_len, dh = q.shape
    scale = dh ** -0.5
    outs = []
    for i in range(s_len // Q_BLOCK):
        q0, q1 = i * Q_BLOCK, (i + 1) * Q_BLOCK
        z = jnp.einsum('bhqd,bhkd->bhqk', q[:, :, q0:q1], k[:, :, :q1]).astype(jnp.float32) * scale
        strict = jnp.arange(q0, q1)[:, None] > jnp.arange(q1)[None, :]
        log_stay = jnp.where(strict, jax.nn.log_sigmoid(-z), 0.0)
        later = lax.cumsum(log_stay, axis=3, reverse=True) - log_stay
        weights = jnp.where(strict, jnp.exp(jax.nn.log_sigmoid(z) + later), 0.0)
        outs.append(jnp.einsum('bhqk,bhkd->bhqd', weights.astype(v.dtype), v[:, :, :q1]))
    return jnp.concatenate(outs, axis=2)


def hybrid_mixer(h, w_in, a_mu, a_w0, a_w2, a_a0, a_a2, a_g2, a_kk_scale, a_ka, a_rk,
                 a_gn_w, a_gn_b, b_fbias, merge_bias, w_br_a, w_br_b, w_br_c, w_out):
    b, s, _ = h.shape
    proj = jnp.einsum('bsd,dn->bsn', h, w_in)
    pa, pb, pc, pg = jnp.split(proj, MIX_SPLITS, axis=-1)
    y_a = rwkv7_time_mix(pa, a_mu, a_w0, a_w2, a_a0, a_a2, a_g2, a_kk_scale, a_ka, a_rk,
                         a_gn_w, a_gn_b)
    qb, kb, vb, fb = jnp.split(pb, (B_WIDTH, 2 * B_WIDTH, 3 * B_WIDTH), axis=-1)
    y_b = from_heads(fox_attention(to_heads(qb, B_HEADS, B_HEAD_DIM), to_heads(kb, B_HEADS, B_HEAD_DIM),
                                   to_heads(vb, B_HEADS, B_HEAD_DIM), fb + b_fbias))
    qc, kc, vc = jnp.split(pc, 3, axis=-1)
    y_c = from_heads(stick_breaking_attention(to_heads(qc, C_HEADS, C_HEAD_DIM),
                                              to_heads(kc, C_HEADS, C_HEAD_DIM),
                                              to_heads(vc, C_HEADS, C_HEAD_DIM)))
    gates = jax.nn.sigmoid(pg.reshape(b, s, N_BRANCH, D_MODEL) + merge_bias)
    merged = (gates[:, :, 0] * (y_a @ w_br_a) + gates[:, :, 1] * (y_b @ w_br_b)
              + gates[:, :, 2] * (y_c @ w_br_c))
    return merged @ w_out


def conv_glu_ffn(h, w_up, conv_w, conv_b, w_down):
    u, v = jnp.split(h @ w_up, 2, axis=-1)
    u = lax.conv_general_dilated(u, conv_w[:, None, :], window_strides=(1,),
                                 padding=((CONV_WIDTH - 1, 0),),
                                 dimension_numbers=('NWC', 'WIO', 'NWC'),
                                 feature_group_count=D_FF) + conv_b
    return (jax.nn.gelu(u, approximate=False) * v) @ w_down


def setup_inputs(seed: int = 0) -> dict:
    key = jax.random.key(seed)
    ks = jax.random.split(key, 30)
    L, D = DEPTH, D_MODEL

    def nrm(k, shape, scale):
        return jax.random.normal(k, shape, jnp.float32) * scale

    return {
        "x": nrm(ks[0], (BATCH, SEQ, D), 1.0),
        "c": nrm(ks[1], (BATCH, D), 1.0),
        "w_ada": nrm(ks[2], (D, N_MOD * D), 0.5 * D ** -0.5),
        "b_ada": nrm(ks[3], (N_MOD * D,), 0.02),
        "ada_table": nrm(ks[4], (L, N_MOD, D), 0.1),
        "norm_mix": 1.0 + nrm(ks[5], (L, D), 0.05),
        "w_in": nrm(ks[6], (L, D, IN_COLS), D ** -0.5),
        "a_mu": jax.random.uniform(ks[7], (L, A_SHIFT_COLS), jnp.float32, 0.0, 1.0),
        "a_w0": jax.random.uniform(ks[8], (L, A_WIDTH), jnp.float32, -4.0, 1.0),
        "a_w2": nrm(ks[9], (L, A_DECAY_LORA, A_WIDTH), 0.5 * A_DECAY_LORA ** -0.5),
        "a_a0": nrm(ks[10], (L, A_WIDTH), 0.1),
        "a_a2": nrm(ks[11], (L, A_ICLR_LORA, A_WIDTH), A_ICLR_LORA ** -0.5),
        "a_g2": nrm(ks[12], (L, A_GATE_LORA, A_WIDTH), A_GATE_LORA ** -0.5),
        "a_kk_scale": 0.85 + nrm(ks[13], (L, A_HEADS, A_HEAD_DIM), 0.05),
        "a_ka": 1.0 + nrm(ks[14], (L, A_HEADS, A_HEAD_DIM), 0.05),
        "a_rk": nrm(ks[15], (L, A_HEADS, A_HEAD_DIM), 0.1),
        "a_gn_w": 1.0 + nrm(ks[16], (L, A_WIDTH), 0.05),
        "a_gn_b": nrm(ks[17], (L, A_WIDTH), 0.02),
        "b_fbias": 2.0 + nrm(ks[18], (L, B_HEADS), 0.5),
        "merge_bias": nrm(ks[19], (L, N_BRANCH, D), 0.1),
        "w_br_a": nrm(ks[20], (L, A_WIDTH, D), A_WIDTH ** -0.5),
        "w_br_b": nrm(ks[21], (L, B_WIDTH, D), B_WIDTH ** -0.5),
        "w_br_c": nrm(ks[22], (L, C_WIDTH, D), C_WIDTH ** -0.5),
        "w_out": nrm(ks[23], (L, D, D), D ** -0.5),
        "norm_ffn": 1.0 + nrm(ks[24], (L, D), 0.05),
        "w_up": nrm(ks[25], (L, D, 2 * D_FF), D ** -0.5),
        "conv_w": nrm(ks[26], (L, CONV_WIDTH, D_FF), CONV_WIDTH ** -0.5),
        "conv_b": nrm(ks[27], (L, D_FF), 0.02),
        "w_down": nrm(ks[28], (L, D_FF, D), D_FF ** -0.5),
        "norm_final": 1.0 + nrm(ks[29], (D,), 0.05),
    }


def reference(x, c, w_ada, b_ada, ada_table, norm_mix, w_in, a_mu, a_w0, a_w2, a_a0, a_a2,
              a_g2, a_kk_scale, a_ka, a_rk, a_gn_w, a_gn_b, b_fbias, merge_bias, w_br_a,
              w_br_b, w_br_c, w_out, norm_ffn, w_up, conv_w, conv_b, w_down, norm_final):
    b, _, d = x.shape
    mod_shared = (jax.nn.silu(c) @ w_ada + b_ada).reshape(b, N_MOD, d)
    for l in range(DEPTH):
        mod = (mod_shared + ada_table[l][None])[:, :, None, :]
        shift_m, scale_m, gate_m = mod[:, 0], mod[:, 1], mod[:, 2]
        shift_f, scale_f, gate_f = mod[:, 3], mod[:, 4], mod[:, 5]

        h = rms_norm(x, norm_mix[l]) * (1 + scale_m) + shift_m
        x = x + gate_m * hybrid_mixer(h, w_in[l], a_mu[l], a_w0[l], a_w2[l], a_a0[l], a_a2[l],
                                      a_g2[l], a_kk_scale[l], a_ka[l], a_rk[l], a_gn_w[l],
                                      a_gn_b[l], b_fbias[l], merge_bias[l], w_br_a[l],
                                      w_br_b[l], w_br_c[l], w_out[l])

        h = rms_norm(x, norm_ffn[l]) * (1 + scale_f) + shift_f
        x = x + gate_f * conv_glu_ffn(h, w_up[l], conv_w[l], conv_b[l], w_down[l])
    return rms_norm(x, norm_final)
```

```python
import functools

import jax
import jax.numpy as jnp
from jax import lax
from jax.experimental import pallas as pl
from jax.experimental.pallas import tpu as pltpu

NORM_EPS = 1e-6
A_HEAD_DIM = 64
A_GN_EPS = 64e-5
ATTN_HEAD_DIM = 128
RWKV_CHUNK = 64

LANE = 128
V7X_VMEM_LIMIT = 56 * 1024 * 1024

F32 = jnp.float32
BF16 = jnp.bfloat16


def _cparams(sem, vmem=None):
    return pltpu.CompilerParams(dimension_semantics=sem, vmem_limit_bytes=vmem)


def _tile(n, pref):
    if n <= pref:
        return n
    t = (pref // LANE) * LANE
    while n % t:
        t -= LANE
    return t


def _split(x, n):
    parts = []
    for _ in range(n - 1):
        p = x.astype(BF16)
        parts.append(p)
        x = x - p.astype(F32)
    parts.append(x.astype(BF16))
    return parts


def _dot(a, b):
    return jnp.dot(a, b, preferred_element_type=F32)


def _dot_nt(a, b):
    return lax.dot_general(a, b, (((1,), (1,)), ((), ())), preferred_element_type=F32)


def _dot_tn(a, b):
    return lax.dot_general(a, b, (((0,), (0,)), ((), ())), preferred_element_type=F32)


def _dot_x(a, b, passes, dot=_dot):
    if passes == 1:
        return dot(a.astype(BF16), b.astype(BF16))
    a1, a2 = _split(a, 2)
    b1, b2 = _split(b, 2)
    return dot(a1, b1) + (dot(a1, b2) + dot(a2, b1))


def _dot_exact_rhs(a, b01, n=3):
    parts = _split(a, n)
    acc = _dot(parts[0], b01)
    for p in parts[1:]:
        acc = acc + _dot(p, b01)
    return acc


def _dot_exact_lhs(a01, b, n=3):
    parts = _split(b, n)
    acc = _dot(a01, parts[0])
    for p in parts[1:]:
        acc = acc + _dot(a01, p)
    return acc


def _adaln_body(c_ref, w_ref, b_ref, t_ref, o_ref):
    c = c_ref[...]
    sc = c * jax.nn.sigmoid(c)
    base = _dot_x(sc, w_ref[...], 3) + b_ref[...]
    o_ref[...] = base[None, :, :] + t_ref[...][:, None, :]


def _adaln(c, w_ada, b_ada, ada_table):
    b, d = c.shape
    n = w_ada.shape[1]
    depth = ada_table.shape[0]
    tn = _tile(n, 512)
    return pl.pallas_call(
        _adaln_body,
        grid=(n // tn,),
        in_specs=[pl.BlockSpec((b, d), lambda j: (0, 0)),
                  pl.BlockSpec((d, tn), lambda j: (0, j)),
                  pl.BlockSpec((1, tn), lambda j: (0, j)),
                  pl.BlockSpec((depth, tn), lambda j: (0, j))],
        out_specs=pl.BlockSpec((depth, b, tn), lambda j: (0, 0, j)),
        out_shape=jax.ShapeDtypeStruct((depth, b, n), F32),
        compiler_params=_cparams(("parallel",), V7X_VMEM_LIMIT),
        name="adaln",
    )(c, w_ada, b_ada.reshape(1, n), ada_table.reshape(depth, n))


def _norm_mod_body(x_ref, g_ref, sc_ref, sh_ref, o_ref):
    x = x_ref[0]
    y = x * lax.rsqrt(jnp.mean(x * x, axis=-1, keepdims=True) + NORM_EPS) * g_ref[...]
    o_ref[0] = (y * (1.0 + sc_ref[0]) + sh_ref[0]).astype(o_ref.dtype)


def _norm_body(x_ref, g_ref, o_ref):
    x = x_ref[0]
    y = x * lax.rsqrt(jnp.mean(x * x, axis=-1, keepdims=True) + NORM_EPS) * g_ref[...]
    o_ref[0] = y.astype(o_ref.dtype)


def _rms_norm(x, gain, scale=None, shift=None, out_dtype=BF16):
    b, s, d = x.shape
    ts = _tile(s, 256)
    xspec = pl.BlockSpec((1, ts, d), lambda i, j: (i, j, 0))
    gspec = pl.BlockSpec((1, d), lambda i, j: (0, 0))
    mspec = pl.BlockSpec((1, 1, d), lambda i, j: (i, 0, 0))
    if scale is None:
        body, specs, args = _norm_body, [xspec, gspec], (x, gain.reshape(1, d))
    else:
        body, specs = _norm_mod_body, [xspec, gspec, mspec, mspec]
        args = (x, gain.reshape(1, d), scale, shift)
    return pl.pallas_call(
        body, grid=(b, s // ts), in_specs=specs, out_specs=xspec,
        out_shape=jax.ShapeDtypeStruct((b, s, d), out_dtype),
        compiler_params=_cparams(("parallel", "parallel")),
        name="rms_norm",
    )(*args)


def _mm_body(x_ref, w_ref, o_ref):
    o_ref[...] = _dot(x_ref[...], w_ref[...]).astype(o_ref.dtype)


def _matmul(x, w, out_dtype, name, tm_pref=1024, tn_pref=1024):
    m, k = x.shape
    n = w.shape[1]
    tm, tn = _tile(m, tm_pref), _tile(n, tn_pref)
    return pl.pallas_call(
        _mm_body, grid=(m // tm, n // tn),
        in_specs=[pl.BlockSpec((tm, k), lambda i, j: (i, 0)),
                  pl.BlockSpec((k, tn), lambda i, j: (0, j))],
        out_specs=pl.BlockSpec((tm, tn), lambda i, j: (i, j)),
        out_shape=jax.ShapeDtypeStruct((m, n), out_dtype),
        compiler_params=_cparams(("parallel", "parallel"), V7X_VMEM_LIMIT),
        name=name,
    )(x, w)


def _mm_res_body(x_ref, w_ref, r_ref, g_ref, o_ref):
    o_ref[...] = r_ref[...] + g_ref[0] * _dot(x_ref[...], w_ref[...])


def _matmul_residual(x, w, res, gate, seq, name, tm_pref=1024, tn_pref=512):
    m, k = x.shape
    n = w.shape[1]
    tm, tn = _tile(seq, tm_pref), _tile(n, tn_pref)
    per_batch = seq // tm
    return pl.pallas_call(
        _mm_res_body, grid=(m // tm, n // tn),
        in_specs=[pl.BlockSpec((tm, k), lambda i, j: (i, 0)),
                  pl.BlockSpec((k, tn), lambda i, j: (0, j)),
                  pl.BlockSpec((tm, tn), lambda i, j: (i, j)),
                  pl.BlockSpec((1, 1, tn), lambda i, j: (i // per_batch, 0, j))],
        out_specs=pl.BlockSpec((tm, tn), lambda i, j: (i, j)),
        out_shape=jax.ShapeDtypeStruct((m, n), F32),
        compiler_params=_cparams(("parallel", "parallel"), V7X_VMEM_LIMIT),
        name=name,
    )(x, w, res, gate)


def _merge_body(ya_ref, yb_ref, yc_ref, wa_ref, wb_ref, wc_ref,
                ga_ref, gb_ref, gc_ref, bias_ref, o_ref):
    bias = bias_ref[...]
    acc = jax.nn.sigmoid(ga_ref[...] + bias[0]) * _dot(ya_ref[...], wa_ref[...])
    acc = acc + jax.nn.sigmoid(gb_ref[...] + bias[1]) * _dot(yb_ref[...], wb_ref[...])
    acc = acc + jax.nn.sigmoid(gc_ref[...] + bias[2]) * _dot(yc_ref[...], wc_ref[...])
    o_ref[...] = acc.astype(o_ref.dtype)


def _merge(ya, yb, yc, wa, wb, wc, pg, bias, tm_pref=1024, tn_pref=512):
    m = ya.shape[0]
    d = wa.shape[1]
    tm, tn = _tile(m, tm_pref), _tile(d, tn_pref)
    nb = d // tn

    def yspec(y):
        return pl.BlockSpec((tm, y.shape[1]), lambda i, j: (i, 0))

    def wspec(w):
        return pl.BlockSpec((w.shape[0], tn), lambda i, j: (0, j))

    def gspec(k):
        return pl.BlockSpec((tm, tn), lambda i, j: (i, j + k * nb))

    return pl.pallas_call(
        _merge_body, grid=(m // tm, nb),
        in_specs=[yspec(ya), yspec(yb), yspec(yc), wspec(wa), wspec(wb), wspec(wc),
                  gspec(0), gspec(1), gspec(2),
                  pl.BlockSpec((3, 1, tn), lambda i, j: (0, 0, j))],
        out_specs=pl.BlockSpec((tm, tn), lambda i, j: (i, j)),
        out_shape=jax.ShapeDtypeStruct((m, d), BF16),
        compiler_params=_cparams(("parallel", "parallel"), V7X_VMEM_LIMIT),
        name="merge",
    )(ya, yb, yc, wa, wb, wc, pg, pg, pg, bias)


def _head_ones(width):
    r = lax.broadcasted_iota(jnp.int32, (width, width), 0) // A_HEAD_DIM
    c = lax.broadcasted_iota(jnp.int32, (width, width), 1) // A_HEAD_DIM
    return (r == c).astype(BF16)


def _head_sum(x, ones):
    pieces = [_dot_exact_rhs(x[:, p:p + LANE], ones) for p in range(0, x.shape[1], LANE)]
    return pieces[0] if len(pieces) == 1 else jnp.concatenate(pieces, axis=1)


def _rwkv_prep_body(dims, rkv_ref, sm_ref, mu_rkv_ref, mu_sm_ref, w0_ref, w2_ref, a0_ref,
                    a2_ref, g2_ref, kks_ref, ka_ref,
                    r_ref, lw_ref, k_ref, v_ref, a_ref, b_ref, gate_ref,
                    prev_rkv, prev_sm):
    w, lora_pad = dims
    ts = rkv_ref.shape[1]

    @pl.when(pl.program_id(1) == 0)
    def _():
        prev_rkv[...] = jnp.zeros_like(prev_rkv)
        prev_sm[...] = jnp.zeros_like(prev_sm)

    def shifted(cur, prev_ref):
        rolled = pltpu.roll(cur, 1, axis=0)
        row = lax.broadcasted_iota(jnp.int32, cur.shape, 0)
        prev = jnp.where(row == 0, prev_ref[...], rolled)
        prev_ref[...] = cur[ts - 1:ts, :]
        return prev

    rkv = rkv_ref[0]
    sm = sm_ref[0]
    z = rkv + (shifted(rkv, prev_rkv) - rkv) * mu_rkv_ref[...]
    zs = sm + (shifted(sm, prev_sm) - sm) * mu_sm_ref[...]
    r, k, v = z[:, :w], z[:, w:2 * w], z[:, 2 * w:3 * w]
    zl = zs[:, :lora_pad]
    log_w = -jax.nn.softplus(-(w0_ref[...] + _dot_x(jnp.tanh(zl), w2_ref[...], 3))) - 0.5
    iclr = jax.nn.sigmoid(a0_ref[...] + _dot_x(zl, a2_ref[...], 3))
    gate = _dot_x(jax.nn.sigmoid(zl), g2_ref[...], 3)

    ones = _head_ones(LANE)
    kk = k * kks_ref[...]
    norm = jnp.sqrt(_head_sum(kk * kk, ones))
    kk = kk / jnp.maximum(norm, 1e-12)

    r_ref[0] = r
    lw_ref[0] = -jnp.exp(log_w)
    k_ref[0] = k * (1.0 + (iclr - 1.0) * ka_ref[...])
    v_ref[0] = v
    a_ref[0] = -kk
    b_ref[0] = kk * iclr
    gate_ref[0] = gate


def _rwkv_prep(rkv, small, mu_rkv, mu_small, w0, w2, a0, a2, g2, kk_scale, k_a):
    b, s, w3 = rkv.shape
    w = w3 // 3
    ns = small.shape[2]
    dl, il, gl = w2.shape[0], a2.shape[0], g2.shape[0]
    lora_pad = -(-(dl + il + gl) // LANE) * LANE
    w2 = jnp.pad(w2, ((0, lora_pad - dl), (0, 0)))
    a2 = jnp.pad(a2, ((dl, lora_pad - dl - il), (0, 0)))
    g2 = jnp.pad(g2, ((dl + il, lora_pad - dl - il - gl), (0, 0)))
    ts = _tile(s, 256)
    row = lambda arr: arr.reshape(1, -1)
    full = lambda arr: pl.BlockSpec(arr.shape, lambda i, j: (0,) * arr.ndim)
    params = [row(mu_rkv), row(mu_small), row(w0), w2, row(a0), a2, g2, row(kk_scale), row(k_a)]
    out_spec = pl.BlockSpec((1, ts, w), lambda i, j: (i, j, 0))
    out = jax.ShapeDtypeStruct((b, s, w), F32)
    return pl.pallas_call(
        functools.partial(_rwkv_prep_body, (w, lora_pad)),
        grid=(b, s // ts),
        in_specs=[pl.BlockSpec((1, ts, w3), lambda i, j: (i, j, 0)),
                  pl.BlockSpec((1, ts, ns), lambda i, j: (i, j, 0))] + [full(p) for p in params],
        out_specs=[out_spec] * 7,
        out_shape=[out] * 7,
        scratch_shapes=[pltpu.VMEM((1, w3), F32), pltpu.VMEM((1, ns), F32)],
        compiler_params=_cparams(("parallel", "arbitrary"), V7X_VMEM_LIMIT),
        name="rwkv_prep",
    )(rkv, small, *params)


def _rwkv_rec_body(r_ref, lw_ref, k_ref, v_ref, a_ref, b_ref, gate_ref, rk_ref, gnw_ref,
                   gnb_ref, o_ref, state):
    c = RWKV_CHUNK
    n_pairs = r_ref.shape[2] // LANE

    @pl.when(pl.program_id(1) == 0)
    def _():
        state[...] = jnp.zeros_like(state)

    row = lax.broadcasted_iota(jnp.int32, (c, c), 0)
    col = lax.broadcasted_iota(jnp.int32, (c, c), 1)
    tri_incl = (row >= col).astype(BF16)

    r2 = lax.broadcasted_iota(jnp.int32, (2 * c, LANE), 0)
    c2 = lax.broadcasted_iota(jnp.int32, (2 * c, LANE), 1)
    same_head = (r2 // c) == (c2 // A_HEAD_DIM)
    t_row, t_col = r2 % c, c2 % c
    strict = same_head & (t_row > t_col)
    incl = same_head & (t_row >= t_col)
    eye = (r2 == c2).astype(F32)
    ones = _head_ones(LANE)

    def bd(x):
        return jnp.where(same_head, jnp.concatenate([x, x], axis=0), 0.0)

    outs = []
    for p in range(n_pairs):
        sl = slice(p * LANE, (p + 1) * LANE)
        r, lw, k, v = r_ref[0, :, sl], lw_ref[0, :, sl], k_ref[0, :, sl], v_ref[0, :, sl]
        a, b = a_ref[0, :, sl], b_ref[0, :, sl]

        cl = _dot_exact_lhs(tri_incl, lw)
        cl_last = cl[c - 1:c, :]
        e_neg = jnp.exp(-cl)
        xa = bd(a * jnp.exp(cl - lw))
        xr = bd(r * jnp.exp(cl))
        yb = bd(b * e_neg)
        yk = bd(k * e_neg)
        e_end = jnp.exp(cl_last - cl)
        bh, kh, vv = bd(b * e_end), bd(k * e_end), bd(v)
        w_end = jnp.exp(cl_last)

        a_ab = jnp.where(strict, _dot_x(xa, yb, 3, _dot_nt), 0.0)
        a_ak = jnp.where(strict, _dot_x(xa, yk, 3, _dot_nt), 0.0)
        a_rb = jnp.where(incl, _dot_x(xr, yb, 3, _dot_nt), 0.0)
        a_rk = jnp.where(incl, _dot_x(xr, yk, 3, _dot_nt), 0.0)

        t_inv = eye + a_ab
        pw = a_ab
        n = 1
        while 2 * n < c:
            pw = _dot_x(pw, pw, 3)
            t_inv = t_inv + _dot_x(pw, t_inv, 3)
            n *= 2

        pp = _dot_x(t_inv, xa, 3)
        qq = _dot_x(t_inv, _dot_x(a_ak, vv, 3), 3)
        g = xr + _dot_x(a_rb, pp, 3)
        y_loc = _dot_x(a_rb, qq, 3) + _dot_x(a_rk, vv, 3)
        m = eye * w_end + _dot_x(bh, pp, 3, _dot_tn)
        n_loc = _dot_x(bh, qq, 3, _dot_tn) + _dot_x(kh, vv, 3, _dot_tn)

        s0 = state[p]
        y_bd = _dot_x(g, s0, 3) + y_loc
        state[p] = _dot_x(m, s0, 3) + n_loc
        y = y_bd[:c] + y_bd[c:]

        mean = _head_sum(y, ones) * (1.0 / A_HEAD_DIM)
        yc = y - mean
        var = _head_sum(yc * yc, ones) * (1.0 / A_HEAD_DIM)
        yn = yc * lax.rsqrt(var + A_GN_EPS) * gnw_ref[:, sl] + gnb_ref[:, sl]
        bonus = _head_sum(r * k * rk_ref[:, sl], ones) * v
        outs.append((yn + bonus) * gate_ref[0, :, sl])
    o_ref[0] = jnp.concatenate(outs, axis=1).astype(o_ref.dtype)


def _rwkv_recurrence(r, lw, k, v, a, b, gate, r_k, gn_w, gn_b):
    bsz, s, w = r.shape
    c = RWKV_CHUNK
    blk = pl.BlockSpec((1, c, w), lambda i, j: (i, j, 0))
    par = pl.BlockSpec((1, w), lambda i, j: (0, 0))
    return pl.pallas_call(
        _rwkv_rec_body, grid=(bsz, s // c),
        in_specs=[blk] * 7 + [par] * 3,
        out_specs=blk,
        out_shape=jax.ShapeDtypeStruct((bsz, s, w), BF16),
        scratch_shapes=[pltpu.VMEM((w // LANE, LANE, LANE), F32)],
        compiler_params=_cparams(("parallel", "arbitrary")),
        name="rwkv_recurrence",
    )(r, lw, k, v, a, b, gate, r_k.reshape(1, w), gn_w.reshape(1, w), gn_b.reshape(1, w))


def _fox_cum_body(f_ref, bias_ref, cum_ref, cum_t_ref):
    s = f_ref.shape[1]
    t = _tile(s, 256)
    row = lax.broadcasted_iota(jnp.int32, (t, t), 0)
    col = lax.broadcasted_iota(jnp.int32, (t, t), 1)
    tri = (row >= col).astype(BF16)
    carry = jnp.zeros((1, LANE), F32)
    for i in range(s // t):
        ls = jax.nn.log_sigmoid(f_ref[0, i * t:(i + 1) * t, :] + bias_ref[...])
        cum = _dot_exact_lhs(tri, ls) + carry
        carry = cum[t - 1:t, :]
        cum_ref[0, i * t:(i + 1) * t, :] = cum
        cum_t_ref[0, :, i * t:(i + 1) * t] = cum.T


def _fox_cum(small, lane_block, bias):
    b, s, _ = small.shape
    return pl.pallas_call(
        _fox_cum_body, grid=(b,),
        in_specs=[pl.BlockSpec((1, s, LANE), lambda i: (i, 0, lane_block)),
                  pl.BlockSpec((1, LANE), lambda i: (0, 0))],
        out_specs=[pl.BlockSpec((1, s, LANE), lambda i: (i, 0, 0)),
                   pl.BlockSpec((1, LANE, s), lambda i: (i, 0, 0))],
        out_shape=[jax.ShapeDtypeStruct((b, s, LANE), F32),
                   jax.ShapeDtypeStruct((b, LANE, s), F32)],
        compiler_params=_cparams(("parallel",)),
        name="fox_cum",
    )(small, bias)


def _fox_body(t, q_ref, k_ref, v_ref, cum_ref, cum_t_ref, o_ref):
    h = pl.program_id(1)
    qi = pl.program_id(2)
    scale = ATTN_HEAD_DIM ** -0.5
    q = q_ref[0]
    sel = (lax.broadcasted_iota(jnp.int32, (LANE, t), 0) == h).astype(BF16)
    cum_q = _dot_exact_rhs(cum_ref[0], sel)
    row = lax.broadcasted_iota(jnp.int32, (t, t), 0)
    col = lax.broadcasted_iota(jnp.int32, (t, t), 1)

    def step(kj, carry):
        m, l, acc = carry
        k0 = pl.multiple_of(kj * t, t)
        kt = k_ref[0, pl.ds(k0, t), :]
        vt = v_ref[0, pl.ds(k0, t), :]
        cum_k = cum_t_ref[0, 0, :, pl.ds(k0, t)]
        logits = _dot_nt(q, kt) * scale + cum_q - cum_k
        causal = (qi * t + row) >= (k0 + col)
        logits = jnp.where(causal, logits, -jnp.inf)
        m_new = jnp.maximum(m, jnp.max(logits, axis=1, keepdims=True))
        alpha = jnp.exp(m - m_new)
        p = jnp.exp(logits - m_new)
        l = alpha * l + jnp.sum(p, axis=1, keepdims=True)
        acc = alpha * acc + _dot(p.astype(BF16), vt)
        return m_new, l, acc

    init = (jnp.full((t, 1), -jnp.inf, F32), jnp.zeros((t, 1), F32),
            jnp.zeros((t, ATTN_HEAD_DIM), F32))
    m, l, acc = lax.fori_loop(0, qi + 1, step, init)
    o_ref[0] = (acc / l).astype(o_ref.dtype)


def _fox_attention(qkv, cum, cum_t, n_heads):
    b, s, _ = qkv.shape
    t = _tile(s, 128)
    kv = lambda off: pl.BlockSpec((1, s, ATTN_HEAD_DIM), lambda i, h, j: (i, 0, off + h))
    return pl.pallas_call(
        functools.partial(_fox_body, t), grid=(b, n_heads, s // t),
        in_specs=[pl.BlockSpec((1, t, ATTN_HEAD_DIM), lambda i, h, j: (i, j, h)),
                  kv(n_heads), kv(2 * n_heads),
                  pl.BlockSpec((1, t, LANE), lambda i, h, j: (i, j, 0)),
                  pl.BlockSpec((1, 1, 1, s), lambda i, h, j: (i, h, 0, 0))],
        out_specs=pl.BlockSpec((1, t, ATTN_HEAD_DIM), lambda i, h, j: (i, j, h)),
        out_shape=jax.ShapeDtypeStruct((b, s, n_heads * ATTN_HEAD_DIM), BF16),
        compiler_params=_cparams(("parallel", "parallel", "arbitrary")),
        name="fox_attention",
    )(qkv, qkv, qkv, cum, cum_t.reshape(b, LANE, 1, s))


def _sb_body(t, q_ref, k_ref, v_ref, o_ref):
    qi = pl.program_id(2)
    scale = ATTN_HEAD_DIM ** -0.5
    q = q_ref[0]
    row = lax.broadcasted_iota(jnp.int32, (t, t), 0)
    col = lax.broadcasted_iota(jnp.int32, (t, t), 1)
    after = (row > col).astype(BF16)

    def step(i, carry):
        later_rows, acc = carry
        kj = qi - i
        k0 = pl.multiple_of(kj * t, t)
        kt = k_ref[0, pl.ds(k0, t), :]
        vt = v_ref[0, pl.ds(k0, t), :]
        z = _dot_nt(q, kt) * scale
        strict = (qi * t + row) > (k0 + col)
        log_take = jax.nn.log_sigmoid(z)
        log_stay = jnp.where(strict, log_take - z, 0.0)
        later = later_rows + _dot_exact_rhs(log_stay, after, 2)
        wts = jnp.where(strict, jnp.exp(log_take + later), 0.0)
        acc = acc + _dot(wts.astype(BF16), vt)
        later_rows = later_rows + jnp.sum(log_stay, axis=1, keepdims=True)
        return later_rows, acc

    init = (jnp.zeros((t, 1), F32), jnp.zeros((t, ATTN_HEAD_DIM), F32))
    _, acc = lax.fori_loop(0, qi + 1, step, init)
    o_ref[0] = acc.astype(o_ref.dtype)


def _sb_attention(qkv, n_heads):
    b, s, _ = qkv.shape
    t = _tile(s, 128)
    kv = lambda off: pl.BlockSpec((1, s, ATTN_HEAD_DIM), lambda i, h, j: (i, 0, off + h))
    return pl.pallas_call(
        functools.partial(_sb_body, t), grid=(b, n_heads, s // t),
        in_specs=[pl.BlockSpec((1, t, ATTN_HEAD_DIM), lambda i, h, j: (i, j, h)),
                  kv(n_heads), kv(2 * n_heads)],
        out_specs=pl.BlockSpec((1, t, ATTN_HEAD_DIM), lambda i, h, j: (i, j, h)),
        out_shape=jax.ShapeDtypeStruct((b, s, n_heads * ATTN_HEAD_DIM), BF16),
        compiler_params=_cparams(("parallel", "parallel", "arbitrary")),
        name="sb_attention",
    )(qkv, qkv, qkv)


def _glu_body(u_ref, v_ref, cw_ref, cb_ref, o_ref, tail):
    ts = u_ref.shape[1]

    @pl.when(pl.program_id(2) == 0)
    def _():
        tail[...] = jnp.zeros_like(tail)

    u = u_ref[0]
    row = lax.broadcasted_iota(jnp.int32, u.shape, 0)
    prev = tail[...]
    u1 = jnp.where(row == 0, prev[1:2], pltpu.roll(u, 1, axis=0))
    u2 = jnp.where(row == 0, prev[0:1], jnp.where(row == 1, prev[1:2], pltpu.roll(u, 2, axis=0)))
    tail[...] = u[ts - 2:ts, :]
    cw = cw_ref[...]
    conv = u2 * cw[0:1] + u1 * cw[1:2] + u * cw[2:3] + cb_ref[...]
    act = 0.5 * conv * (1.0 + lax.erf(conv * (2.0 ** -0.5)))
    o_ref[0] = (act * v_ref[0]).astype(o_ref.dtype)


def _conv_glu_act(uv, conv_w, conv_b):
    b, s, f2 = uv.shape
    f = f2 // 2
    ts, tf = _tile(s, 512), _tile(f, 512)
    nf = f // tf
    return pl.pallas_call(
        _glu_body, grid=(b, nf, s // ts),
        in_specs=[pl.BlockSpec((1, ts, tf), lambda i, c, j: (i, j, c)),
                  pl.BlockSpec((1, ts, tf), lambda i, c, j: (i, j, c + nf)),
                  pl.BlockSpec((conv_w.shape[0], tf), lambda i, c, j: (0, c)),
                  pl.BlockSpec((1, tf), lambda i, c, j: (0, c))],
        out_specs=pl.BlockSpec((1, ts, tf), lambda i, c, j: (i, j, c)),
        out_shape=jax.ShapeDtypeStruct((b, s, f), BF16),
        scratch_shapes=[pltpu.VMEM((2, tf), F32)],
        compiler_params=_cparams(("parallel", "parallel", "arbitrary")),
        name="conv_glu_act",
    )(uv, uv, conv_w, conv_b.reshape(1, f))


def _pad_cols(w, n):
    return jnp.pad(w, ((0, 0), (0, 0), (0, n - w.shape[2])))


def kernel(x, c, w_ada, b_ada, ada_table, norm_mix, w_in, a_mu, a_w0, a_w2, a_a0, a_a2, a_g2,
           a_kk_scale, a_ka, a_rk, a_gn_w, a_gn_b, b_fbias, merge_bias, w_br_a, w_br_b, w_br_c,
           w_out, norm_ffn, w_up, conv_w, conv_b, w_down, norm_final):
    bsz, seq, d = x.shape
    depth = ada_table.shape[0]
    m = bsz * seq
    aw = a_w0.shape[1]
    lora = a_w2.shape[1] + a_a2.shape[1] + a_g2.shape[1]
    bw, cw = w_br_b.shape[1], w_br_c.shape[1]
    b_heads, c_heads = bw // ATTN_HEAD_DIM, cw // ATTN_HEAD_DIM
    a_cols = 3 * aw + lora
    b_cols = 3 * bw + b_heads
    lora_pad = -(-lora // LANE) * LANE

    w_rkv = w_in[:, :, :3 * aw].astype(BF16)
    w_small = jnp.concatenate(
        [_pad_cols(w_in[:, :, 3 * aw:a_cols], lora_pad),
         _pad_cols(w_in[:, :, a_cols + 3 * bw:a_cols + b_cols], LANE)], axis=2).astype(BF16)
    w_bqkv = w_in[:, :, a_cols:a_cols + 3 * bw].astype(BF16)
    w_cqkv = w_in[:, :, a_cols + b_cols:a_cols + b_cols + 3 * cw].astype(BF16)
    w_gate = w_in[:, :, a_cols + b_cols + 3 * cw:].astype(BF16)
    w_br_a, w_br_b, w_br_c = w_br_a.astype(BF16), w_br_b.astype(BF16), w_br_c.astype(BF16)
    w_out, w_up, w_down = w_out.astype(BF16), w_up.astype(BF16), w_down.astype(BF16)
    mu_rkv = a_mu[:, :3 * aw]
    mu_small = jnp.pad(a_mu[:, 3 * aw:], ((0, 0), (0, lora_pad + LANE - lora)))
    fbias = jnp.pad(b_fbias, ((0, 0), (0, LANE - b_heads)))

    mods = _adaln(c, w_ada, b_ada, ada_table).reshape(depth, bsz, -1, 1, d)

    for l in range(depth):
        shift_m, scale_m, gate_m = mods[l, :, 0], mods[l, :, 1], mods[l, :, 2]
        shift_f, scale_f, gate_f = mods[l, :, 3], mods[l, :, 4], mods[l, :, 5]

        h = _rms_norm(x, norm_mix[l], scale_m, shift_m).reshape(m, d)
        rkv = _matmul(h, w_rkv[l], F32, "proj_rkv").reshape(bsz, seq, -1)
        small = _matmul(h, w_small[l], F32, "proj_small").reshape(bsz, seq, -1)
        bqkv = _matmul(h, w_bqkv[l], BF16, "proj_fox", tn_pref=768).reshape(bsz, seq, -1)
        cqkv = _matmul(h, w_cqkv[l], BF16, "proj_sb", tn_pref=768).reshape(bsz, seq, -1)
        pg = _matmul(h, w_gate[l], F32, "proj_gate")

        r, lw, k, v, a, b, gate = _rwkv_prep(
            rkv, small, mu_rkv[l], mu_small[l], a_w0[l], a_w2[l], a_a0[l], a_a2[l], a_g2[l],
            a_kk_scale[l].reshape(-1), a_ka[l].reshape(-1))
        y_a = _rwkv_recurrence(r, lw, k, v, a, b, gate, a_rk[l].reshape(-1), a_gn_w[l], a_gn_b[l])

        cum, cum_t = _fox_cum(small, lora_pad // LANE, fbias[l].reshape(1, LANE))
        y_b = _fox_attention(bqkv, cum, cum_t, b_heads)
        y_c = _sb_attention(cqkv, c_heads)

        merged = _merge(y_a.reshape(m, aw), y_b.reshape(m, bw), y_c.reshape(m, cw),
                        w_br_a[l], w_br_b[l], w_br_c[l], pg, merge_bias[l].reshape(3, 1, d))
        x = _matmul_residual(merged, w_out[l], x.reshape(m, d), gate_m, seq,
                             "out_proj").reshape(bsz, seq, d)

        h = _rms_norm(x, norm_ffn[l], scale_f, shift_f).reshape(m, d)
        uv = _matmul(h, w_up[l], F32, "ffn_up").reshape(bsz, seq, -1)
        act = _conv_glu_act(uv, conv_w[l], conv_b[l]).reshape(m, -1)
        x = _matmul_residual(act, w_down[l], x.reshape(m, d), gate_f, seq,
                             "ffn_down").reshape(bsz, seq, d)

    return _rms_norm(x, norm_final, out_dtype=x.dtype)
```

```python
import functools

import jax
import jax.numpy as jnp
from jax import lax
from jax.experimental import pallas as pl
from jax.experimental.pallas import tpu as pltpu

NORM_EPS = 1e-6
A_HEAD_DIM = 64
A_GN_EPS = 64e-5
ATTN_HEAD_DIM = 128
ATTN_Q_TILE = 256
SB_HEAD_GROUP = 2
RWKV_CHUNK = 64
RWKV_PASSES = (1, 1, 1, 1)

LANE = 128
V7X_VMEM_LIMIT = 56 * 1024 * 1024

F32 = jnp.float32
BF16 = jnp.bfloat16


def _cparams(sem, vmem=None):
    return pltpu.CompilerParams(dimension_semantics=sem, vmem_limit_bytes=vmem)


def _tile(n, pref):
    if n <= pref:
        return n
    t = (pref // LANE) * LANE
    while n % t:
        t -= LANE
    return t


def _split(x, n):
    parts = []
    for _ in range(n - 1):
        p = x.astype(BF16)
        parts.append(p)
        x = x - p.astype(F32)
    parts.append(x.astype(BF16))
    return parts


def _dot(a, b):
    return jnp.dot(a, b, preferred_element_type=F32)


def _dot_nt(a, b):
    return lax.dot_general(a, b, (((1,), (1,)), ((), ())), preferred_element_type=F32)


def _dot_tn(a, b):
    return lax.dot_general(a, b, (((0,), (0,)), ((), ())), preferred_element_type=F32)


def _dot_x(a, b, passes, dot=_dot):
    if passes == 1:
        return dot(a.astype(BF16), b.astype(BF16))
    a1, a2 = _split(a, 2)
    b1, b2 = _split(b, 2)
    return dot(a1, b1) + (dot(a1, b2) + dot(a2, b1))


def _dot_exact_rhs(a, b01, n=3):
    parts = _split(a, n)
    acc = _dot(parts[0], b01)
    for p in parts[1:]:
        acc = acc + _dot(p, b01)
    return acc


def _dot_exact_lhs(a01, b, n=3):
    parts = _split(b, n)
    acc = _dot(a01, parts[0])
    for p in parts[1:]:
        acc = acc + _dot(a01, p)
    return acc


def _adaln_body(c_ref, w_ref, b_ref, t_ref, o_ref):
    c = c_ref[...]
    sc = c * jax.nn.sigmoid(c)
    base = _dot_x(sc, w_ref[...], 3) + b_ref[...]
    o_ref[...] = base[None, :, :] + t_ref[...][:, None, :]


def _adaln(c, w_ada, b_ada, ada_table):
    b, d = c.shape
    n = w_ada.shape[1]
    depth = ada_table.shape[0]
    tn = _tile(n, 512)
    return pl.pallas_call(
        _adaln_body,
        grid=(n // tn,),
        in_specs=[pl.BlockSpec((b, d), lambda j: (0, 0)),
                  pl.BlockSpec((d, tn), lambda j: (0, j)),
                  pl.BlockSpec((1, tn), lambda j: (0, j)),
                  pl.BlockSpec((depth, tn), lambda j: (0, j))],
        out_specs=pl.BlockSpec((depth, b, tn), lambda j: (0, 0, j)),
        out_shape=jax.ShapeDtypeStruct((depth, b, n), F32),
        compiler_params=_cparams(("parallel",), V7X_VMEM_LIMIT),
        name="adaln",
    )(c, w_ada, b_ada.reshape(1, n), ada_table.reshape(depth, n))


def _norm_mod_body(x_ref, g_ref, sc_ref, sh_ref, o_ref):
    x = x_ref[0]
    y = x * lax.rsqrt(jnp.mean(x * x, axis=-1, keepdims=True) + NORM_EPS) * g_ref[...]
    o_ref[0] = (y * (1.0 + sc_ref[0]) + sh_ref[0]).astype(o_ref.dtype)


def _norm_body(x_ref, g_ref, o_ref):
    x = x_ref[0]
    y = x * lax.rsqrt(jnp.mean(x * x, axis=-1, keepdims=True) + NORM_EPS) * g_ref[...]
    o_ref[0] = y.astype(o_ref.dtype)


def _rms_norm(x, gain, scale=None, shift=None, out_dtype=BF16):
    b, s, d = x.shape
    ts = _tile(s, 256)
    xspec = pl.BlockSpec((1, ts, d), lambda i, j: (i, j, 0))
    gspec = pl.BlockSpec((1, d), lambda i, j: (0, 0))
    mspec = pl.BlockSpec((1, 1, d), lambda i, j: (i, 0, 0))
    if scale is None:
        body, specs, args = _norm_body, [xspec, gspec], (x, gain.reshape(1, d))
    else:
        body, specs = _norm_mod_body, [xspec, gspec, mspec, mspec]
        args = (x, gain.reshape(1, d), scale, shift)
    return pl.pallas_call(
        body, grid=(b, s // ts), in_specs=specs, out_specs=xspec,
        out_shape=jax.ShapeDtypeStruct((b, s, d), out_dtype),
        compiler_params=_cparams(("parallel", "parallel")),
        name="rms_norm",
    )(*args)


def _mm_body(x_ref, w_ref, o_ref):
    o_ref[...] = _dot(x_ref[...], w_ref[...]).astype(o_ref.dtype)


def _wspec(w, layer, tn):
    return pl.BlockSpec((None, w.shape[1], tn), lambda i, j: (layer, 0, j))


def _matmul(x, w, layer, out_dtype, name, tm_pref=1024, tn_pref=1024):
    m, k = x.shape
    n = w.shape[2]
    tm, tn = _tile(m, tm_pref), _tile(n, tn_pref)
    return pl.pallas_call(
        _mm_body, grid=(m // tm, n // tn),
        in_specs=[pl.BlockSpec((tm, k), lambda i, j: (i, 0)), _wspec(w, layer, tn)],
        out_specs=pl.BlockSpec((tm, tn), lambda i, j: (i, j)),
        out_shape=jax.ShapeDtypeStruct((m, n), out_dtype),
        compiler_params=_cparams(("parallel", "parallel"), V7X_VMEM_LIMIT),
        name=name,
    )(x, w)


def _mm_res_body(x_ref, w_ref, r_ref, g_ref, o_ref):
    o_ref[...] = r_ref[...] + g_ref[0] * _dot(x_ref[...], w_ref[...])


def _matmul_residual(x, w, layer, res, gate, seq, name, tm_pref=1024, tn_pref=512):
    m, k = x.shape
    n = w.shape[2]
    tm, tn = _tile(seq, tm_pref), _tile(n, tn_pref)
    per_batch = seq // tm
    return pl.pallas_call(
        _mm_res_body, grid=(m // tm, n // tn),
        in_specs=[pl.BlockSpec((tm, k), lambda i, j: (i, 0)),
                  _wspec(w, layer, tn),
                  pl.BlockSpec((tm, tn), lambda i, j: (i, j)),
                  pl.BlockSpec((1, 1, tn), lambda i, j: (i // per_batch, 0, j))],
        out_specs=pl.BlockSpec((tm, tn), lambda i, j: (i, j)),
        out_shape=jax.ShapeDtypeStruct((m, n), F32),
        compiler_params=_cparams(("parallel", "parallel"), V7X_VMEM_LIMIT),
        name=name,
    )(x, w, res, gate)


def _merge_body(ya_ref, yb_ref, yc_ref, wa_ref, wb_ref, wc_ref,
                ga_ref, gb_ref, gc_ref, bias_ref, o_ref):
    bias = bias_ref[...]
    acc = jax.nn.sigmoid(ga_ref[...] + bias[0]) * _dot(ya_ref[...], wa_ref[...])
    acc = acc + jax.nn.sigmoid(gb_ref[...] + bias[1]) * _dot(yb_ref[...], wb_ref[...])
    acc = acc + jax.nn.sigmoid(gc_ref[...] + bias[2]) * _dot(yc_ref[...], wc_ref[...])
    o_ref[...] = acc.astype(o_ref.dtype)


def _merge(ya, yb, yc, wa, wb, wc, layer, pg, bias, tm_pref=1024, tn_pref=512):
    m = ya.shape[0]
    d = wa.shape[2]
    tm, tn = _tile(m, tm_pref), _tile(d, tn_pref)
    nb = d // tn

    def yspec(y):
        return pl.BlockSpec((tm, y.shape[1]), lambda i, j: (i, 0))

    def gspec(k):
        return pl.BlockSpec((tm, tn), lambda i, j: (i, j + k * nb))

    return pl.pallas_call(
        _merge_body, grid=(m // tm, nb),
        in_specs=[yspec(ya), yspec(yb), yspec(yc),
                  _wspec(wa, layer, tn), _wspec(wb, layer, tn), _wspec(wc, layer, tn),
                  gspec(0), gspec(1), gspec(2),
                  pl.BlockSpec((None, 3, 1, tn), lambda i, j: (layer, 0, 0, j))],
        out_specs=pl.BlockSpec((tm, tn), lambda i, j: (i, j)),
        out_shape=jax.ShapeDtypeStruct((m, d), BF16),
        compiler_params=_cparams(("parallel", "parallel"), V7X_VMEM_LIMIT),
        name="merge",
    )(ya, yb, yc, wa, wb, wc, pg, pg, pg, bias)


def _head_ones(width):
    r = lax.broadcasted_iota(jnp.int32, (width, width), 0) // A_HEAD_DIM
    c = lax.broadcasted_iota(jnp.int32, (width, width), 1) // A_HEAD_DIM
    return (r == c).astype(BF16)


def _head_sum(x, ones):
    pieces = [_dot_exact_rhs(x[:, p:p + LANE], ones) for p in range(0, x.shape[1], LANE)]
    return pieces[0] if len(pieces) == 1 else jnp.concatenate(pieces, axis=1)


def _rwkv_prep_body(dims, rkv_ref, sm_ref, mu_rkv_ref, mu_sm_ref, w0_ref, w2_ref, a0_ref,
                    a2_ref, g2_ref, kks_ref, ka_ref,
                    r_ref, lw_ref, k_ref, v_ref, a_ref, b_ref, gate_ref,
                    prev_rkv, prev_sm):
    w, lora_pad = dims
    ts = rkv_ref.shape[1]

    @pl.when(pl.program_id(1) == 0)
    def _():
        prev_rkv[...] = jnp.zeros_like(prev_rkv)
        prev_sm[...] = jnp.zeros_like(prev_sm)

    def shifted(cur, prev_ref):
        rolled = pltpu.roll(cur, 1, axis=0)
        row = lax.broadcasted_iota(jnp.int32, cur.shape, 0)
        prev = jnp.where(row == 0, prev_ref[...], rolled)
        prev_ref[...] = cur[ts - 1:ts, :]
        return prev

    rkv = rkv_ref[0]
    sm = sm_ref[0]
    z = rkv + (shifted(rkv, prev_rkv) - rkv) * mu_rkv_ref[...]
    zs = sm + (shifted(sm, prev_sm) - sm) * mu_sm_ref[...]
    r, k, v = z[:, :w], z[:, w:2 * w], z[:, 2 * w:3 * w]
    zl = zs[:, :lora_pad]
    log_w = -jax.nn.softplus(-(w0_ref[...] + _dot_x(jnp.tanh(zl), w2_ref[...], 3))) - 0.5
    iclr = jax.nn.sigmoid(a0_ref[...] + _dot_x(zl, a2_ref[...], 3))
    gate = _dot_x(jax.nn.sigmoid(zl), g2_ref[...], 3)

    ones = _head_ones(LANE)
    kk = k * kks_ref[...]
    norm = jnp.sqrt(_head_sum(kk * kk, ones))
    kk = kk / jnp.maximum(norm, 1e-12)

    r_ref[0] = r
    lw_ref[0] = -jnp.exp(log_w)
    k_ref[0] = k * (1.0 + (iclr - 1.0) * ka_ref[...])
    v_ref[0] = v
    a_ref[0] = -kk
    b_ref[0] = kk * iclr
    gate_ref[0] = gate


def _rwkv_prep(rkv, small, mu_rkv, mu_small, w0, w2, a0, a2, g2, kk_scale, k_a):
    b, s, w3 = rkv.shape
    w = w3 // 3
    ns = small.shape[2]
    dl, il, gl = w2.shape[0], a2.shape[0], g2.shape[0]
    lora_pad = -(-(dl + il + gl) // LANE) * LANE
    w2 = jnp.pad(w2, ((0, lora_pad - dl), (0, 0)))
    a2 = jnp.pad(a2, ((dl, lora_pad - dl - il), (0, 0)))
    g2 = jnp.pad(g2, ((dl + il, lora_pad - dl - il - gl), (0, 0)))
    ts = _tile(s, 256)
    row = lambda arr: arr.reshape(1, -1)
    full = lambda arr: pl.BlockSpec(arr.shape, lambda i, j: (0,) * arr.ndim)
    params = [row(mu_rkv), row(mu_small), row(w0), w2, row(a0), a2, g2, row(kk_scale), row(k_a)]
    out_spec = pl.BlockSpec((1, ts, w), lambda i, j: (i, j, 0))
    out = jax.ShapeDtypeStruct((b, s, w), F32)
    return pl.pallas_call(
        functools.partial(_rwkv_prep_body, (w, lora_pad)),
        grid=(b, s // ts),
        in_specs=[pl.BlockSpec((1, ts, w3), lambda i, j: (i, j, 0)),
                  pl.BlockSpec((1, ts, ns), lambda i, j: (i, j, 0))] + [full(p) for p in params],
        out_specs=[out_spec] * 7,
        out_shape=[out] * 7,
        scratch_shapes=[pltpu.VMEM((1, w3), F32), pltpu.VMEM((1, ns), F32)],
        compiler_params=_cparams(("parallel", "arbitrary"), V7X_VMEM_LIMIT),
        name="rwkv_prep",
    )(rkv, small, *params)


def _rwkv_rec_body(r_ref, lw_ref, k_ref, v_ref, a_ref, b_ref, gate_ref, rk_ref, gnw_ref,
                   gnb_ref, o_ref, state):
    c = RWKV_CHUNK
    n_pairs = r_ref.shape[2] // LANE

    @pl.when(pl.program_id(1) == 0)
    def _():
        state[...] = jnp.zeros_like(state)

    row = lax.broadcasted_iota(jnp.int32, (c, c), 0)
    col = lax.broadcasted_iota(jnp.int32, (c, c), 1)
    tri_incl = (row >= col).astype(BF16)

    r2 = lax.broadcasted_iota(jnp.int32, (2 * c, LANE), 0)
    c2 = lax.broadcasted_iota(jnp.int32, (2 * c, LANE), 1)
    same_head = (r2 // c) == (c2 // A_HEAD_DIM)
    t_row, t_col = r2 % c, c2 % c
    strict = same_head & (t_row > t_col)
    incl = same_head & (t_row >= t_col)
    eye = (r2 == c2).astype(F32)
    ones = _head_ones(LANE)

    def bd(x):
        return jnp.where(same_head, jnp.concatenate([x, x], axis=0), 0.0)

    cat0 = lambda *xs: jnp.concatenate(xs, axis=0)
    cat1 = lambda *xs: jnp.concatenate(xs, axis=1)
    hh = 2 * c
    pairs = range(n_pairs)
    sls = [slice(p * LANE, (p + 1) * LANE) for p in pairs]
    each = lambda fn, *cols: [fn(*args) for args in zip(*cols)]
    load = lambda ref: [ref[0, :, sl] for sl in sls]
    r, lw, k, v, a, b = (load(ref) for ref in (r_ref, lw_ref, k_ref, v_ref, a_ref, b_ref))

    cl = each(lambda x: _dot_exact_lhs(tri_incl, x), lw)
    cl_last = each(lambda x: x[c - 1:c, :], cl)
    e_neg = each(lambda x: jnp.exp(-x), cl)
    xa = each(lambda a_, cl_, lw_: bd(a_ * jnp.exp(cl_ - lw_)), a, cl, lw)
    xr = each(lambda r_, cl_: bd(r_ * jnp.exp(cl_)), r, cl)
    yb = each(lambda b_, e_: bd(b_ * e_), b, e_neg)
    yk = each(lambda k_, e_: bd(k_ * e_), k, e_neg)
    e_end = each(lambda last, cl_: jnp.exp(last - cl_), cl_last, cl)
    bh = each(lambda b_, e_: bd(b_ * e_), b, e_end)
    kh = each(lambda k_, e_: bd(k_ * e_), k, e_end)
    vv = each(bd, v)
    w_end = each(jnp.exp, cl_last)

    amat = each(lambda xa_, xr_, yb_, yk_: _dot_x(cat0(xa_, xr_), cat0(yb_, yk_),
                                                  RWKV_PASSES[0], _dot_nt), xa, xr, yb, yk)
    a_ab = each(lambda m_: jnp.where(strict, m_[:hh, :hh], 0.0), amat)
    a_ak = each(lambda m_: jnp.where(strict, m_[:hh, hh:], 0.0), amat)
    a_rb = each(lambda m_: jnp.where(incl, m_[hh:, :hh], 0.0), amat)
    a_rk = each(lambda m_: jnp.where(incl, m_[hh:, hh:], 0.0), amat)

    pt = RWKV_PASSES[1]
    t_inv = each(lambda x: eye + x, a_ab)
    pw = each(lambda x: _dot_x(x, x, pt), a_ab)
    n = 2
    while 2 * n < c:
        both = each(lambda pw_, t_: _dot_x(pw_, cat1(pw_, t_), pt), pw, t_inv)
        pw = each(lambda x: x[:, :hh], both)
        t_inv = each(lambda t_, x: t_ + x[:, hh:], t_inv, both)
        n *= 2
    t_inv = each(lambda pw_, t_: t_ + _dot_x(pw_, t_, pt), pw, t_inv)

    po = RWKV_PASSES[2]
    akv = each(lambda x, y: _dot_x(x, y, po), a_ak, vv)
    pq = each(lambda t_, xa_, akv_: _dot_x(t_, cat1(xa_, akv_), po), t_inv, xa, akv)
    rhs = each(lambda pq_, vv_: cat0(pq_, cat1(jnp.zeros_like(vv_), vv_)), pq, vv)
    ytop = each(lambda x, y, rhs_: _dot_x(cat1(x, y), rhs_, po), a_rb, a_rk, rhs)
    sbot = each(lambda x, y, rhs_: _dot_x(cat0(x, y), rhs_, po, _dot_tn), bh, kh, rhs)
    g = each(lambda xr_, t: xr_ + t[:, :hh], xr, ytop)
    m = each(lambda w, t: eye * w + t[:, :hh], w_end, sbot)

    gm = [_dot_x(cat0(g[p], m[p]), state[p], RWKV_PASSES[3]) for p in pairs]
    for p in pairs:
        state[p] = gm[p][hh:] + sbot[p][:, hh:]
    y_bd = each(lambda gm_, t: gm_[:hh] + t[:, hh:], gm, ytop)
    y = each(lambda x: x[:c] + x[c:], y_bd)

    mean = each(lambda x: _head_sum(x, ones) * (1.0 / A_HEAD_DIM), y)
    yc = each(lambda x, mu: x - mu, y, mean)
    var = each(lambda x: _head_sum(x * x, ones) * (1.0 / A_HEAD_DIM), yc)
    bonus = [_head_sum(r[p] * k[p] * rk_ref[:, sls[p]], ones) * v[p] for p in pairs]
    outs = [(yc[p] * lax.rsqrt(var[p] + A_GN_EPS) * gnw_ref[:, sls[p]] + gnb_ref[:, sls[p]]
             + bonus[p]) * gate_ref[0, :, sls[p]] for p in pairs]
    o_ref[0] = jnp.concatenate(outs, axis=1).astype(o_ref.dtype)


def _rwkv_recurrence(r, lw, k, v, a, b, gate, r_k, gn_w, gn_b):
    bsz, s, w = r.shape
    c = RWKV_CHUNK
    blk = pl.BlockSpec((1, c, w), lambda i, j: (i, j, 0))
    par = pl.BlockSpec((1, w), lambda i, j: (0, 0))
    return pl.pallas_call(
        _rwkv_rec_body, grid=(bsz, s // c),
        in_specs=[blk] * 7 + [par] * 3,
        out_specs=blk,
        out_shape=jax.ShapeDtypeStruct((bsz, s, w), BF16),
        scratch_shapes=[pltpu.VMEM((w // LANE, LANE, LANE), F32)],
        compiler_params=_cparams(("parallel", "arbitrary")),
        name="rwkv_recurrence",
    )(r, lw, k, v, a, b, gate, r_k.reshape(1, w), gn_w.reshape(1, w), gn_b.reshape(1, w))


def _fox_cum_body(f_ref, bias_ref, cum_ref, cum_t_ref):
    s = f_ref.shape[1]
    t = _tile(s, 256)
    row = lax.broadcasted_iota(jnp.int32, (t, t), 0)
    col = lax.broadcasted_iota(jnp.int32, (t, t), 1)
    tri = (row >= col).astype(BF16)
    carry = jnp.zeros((1, LANE), F32)
    for i in range(s // t):
        ls = jax.nn.log_sigmoid(f_ref[0, i * t:(i + 1) * t, :] + bias_ref[...])
        cum = _dot_exact_lhs(tri, ls) + carry
        carry = cum[t - 1:t, :]
        cum_ref[0, i * t:(i + 1) * t, :] = cum
        cum_t_ref[0, :, i * t:(i + 1) * t] = cum.T


def _fox_cum(small, lane_block, bias):
    b, s, _ = small.shape
    return pl.pallas_call(
        _fox_cum_body, grid=(b,),
        in_specs=[pl.BlockSpec((1, s, LANE), lambda i: (i, 0, lane_block)),
                  pl.BlockSpec((1, LANE), lambda i: (0, 0))],
        out_specs=[pl.BlockSpec((1, s, LANE), lambda i: (i, 0, 0)),
                   pl.BlockSpec((1, LANE, s), lambda i: (i, 0, 0))],
        out_shape=[jax.ShapeDtypeStruct((b, s, LANE), F32),
                   jax.ShapeDtypeStruct((b, LANE, s), F32)],
        compiler_params=_cparams(("parallel",)),
        name="fox_cum",
    )(small, bias)


def _fox_body(tq, q_ref, k_ref, v_ref, cum_ref, cum_t_ref, o_ref):
    h = pl.program_id(1)
    s = q_ref.shape[1]
    scale = ATTN_HEAD_DIM ** -0.5
    sel = (lax.broadcasted_iota(jnp.int32, (LANE, LANE), 0) == h).astype(BF16)
    row = lax.broadcasted_iota(jnp.int32, (tq, tq), 0)
    col = lax.broadcasted_iota(jnp.int32, (tq, tq), 1)
    causal = row >= col

    for qi in range(s // tq):
        q0, q1 = qi * tq, (qi + 1) * tq
        q = q_ref[0, q0:q1, :]
        cum_q = _dot_exact_rhs(cum_ref[0, q0:q1, :], sel)

        def logits(k0, k1):
            bias_q = jnp.concatenate([cum_q] * ((k1 - k0) // LANE), axis=1)
            return _dot_nt(q, k_ref[0, k0:k1, :]) * scale + bias_q - cum_t_ref[0, 0, :, k0:k1]

        s_diag = jnp.where(causal, logits(q0, q1), -jnp.inf)
        m = jnp.max(s_diag, axis=1, keepdims=True)
        if qi:
            s_past = logits(0, q0)
            m = jnp.maximum(m, jnp.max(s_past, axis=1, keepdims=True))
        p = jnp.exp(s_diag - m)
        l = jnp.sum(p, axis=1, keepdims=True)
        acc = _dot(p.astype(BF16), v_ref[0, q0:q1, :])
        if qi:
            p = jnp.exp(s_past - m)
            l = l + jnp.sum(p, axis=1, keepdims=True)
            acc = acc + _dot(p.astype(BF16), v_ref[0, :q0, :])
        o_ref[0, q0:q1, :] = (acc / l).astype(o_ref.dtype)


def _fox_attention(qkv, cum, cum_t, n_heads):
    b, s, _ = qkv.shape
    tq = _tile(s, ATTN_Q_TILE)
    blk = lambda off: pl.BlockSpec((1, s, ATTN_HEAD_DIM), lambda i, h: (i, 0, off + h))
    return pl.pallas_call(
        functools.partial(_fox_body, tq), grid=(b, n_heads),
        in_specs=[blk(0), blk(n_heads), blk(2 * n_heads),
                  pl.BlockSpec((1, s, LANE), lambda i, h: (i, 0, 0)),
                  pl.BlockSpec((1, 1, 1, s), lambda i, h: (i, h, 0, 0))],
        out_specs=blk(0),
        out_shape=jax.ShapeDtypeStruct((b, s, n_heads * ATTN_HEAD_DIM), BF16),
        compiler_params=_cparams(("parallel", "parallel"), V7X_VMEM_LIMIT),
        name="fox_attention",
    )(qkv, qkv, qkv, cum, cum_t.reshape(b, LANE, 1, s))


def _log_sigmoid(z):
    return jnp.minimum(z, 0.0) - jnp.log(1.0 + jnp.exp(-jnp.abs(z)))


def _sb_body(t, q_ref, k_ref, v_ref, o_ref):
    s = q_ref.shape[1]
    dh = ATTN_HEAD_DIM
    heads = [slice(i * dh, (i + 1) * dh) for i in range(q_ref.shape[2] // dh)]
    each = lambda fn, *cols: [fn(*args) for args in zip(*cols)]
    scale = dh ** -0.5
    row = lax.broadcasted_iota(jnp.int32, (t, t), 0)
    col = lax.broadcasted_iota(jnp.int32, (t, t), 1)
    strict = row > col
    after = strict.astype(BF16)

    for qi in range(s // t):
        rows = slice(qi * t, (qi + 1) * t)
        q = [q_ref[0, rows, h] for h in heads]
        later_rows = [jnp.zeros((t, 1), F32) for _ in heads]
        acc = [jnp.zeros((t, dh), F32) for _ in heads]
        for kj in range(qi, -1, -1):
            keys = slice(kj * t, (kj + 1) * t)
            z = [_dot_nt(q_, k_ref[0, keys, h]) * scale for q_, h in zip(q, heads)]
            log_take = each(_log_sigmoid, z)
            log_stay = each(lambda lt, z_: lt - z_, log_take, z)
            if kj == qi:
                log_stay = each(lambda x: jnp.where(strict, x, 0.0), log_stay)
            later = each(lambda lr, ls: lr + _dot_exact_rhs(ls, after, 2), later_rows, log_stay)
            wts = each(lambda lt, la: jnp.exp(lt + la), log_take, later)
            if kj == qi:
                wts = each(lambda x: jnp.where(strict, x, 0.0), wts)
            acc = [a_ + _dot(w_.astype(BF16), v_ref[0, keys, h])
                   for a_, w_, h in zip(acc, wts, heads)]
            if kj:
                later_rows = each(lambda lr, ls: lr + jnp.sum(ls, axis=1, keepdims=True),
                                  later_rows, log_stay)
        for a_, h in zip(acc, heads):
            o_ref[0, rows, h] = a_.astype(o_ref.dtype)


def _sb_attention(qkv, n_heads):
    b, s, _ = qkv.shape
    t = _tile(s, ATTN_Q_TILE)
    group = SB_HEAD_GROUP if n_heads % SB_HEAD_GROUP == 0 else 1
    ng = n_heads // group
    blk = lambda off: pl.BlockSpec((1, s, group * ATTN_HEAD_DIM), lambda i, h: (i, 0, off + h))
    return pl.pallas_call(
        functools.partial(_sb_body, t), grid=(b, ng),
        in_specs=[blk(0), blk(ng), blk(2 * ng)],
        out_specs=blk(0),
        out_shape=jax.ShapeDtypeStruct((b, s, n_heads * ATTN_HEAD_DIM), BF16),
        compiler_params=_cparams(("parallel", "parallel"), V7X_VMEM_LIMIT),
        name="sb_attention",
    )(qkv, qkv, qkv)


def _glu_body(u_ref, v_ref, cw_ref, cb_ref, o_ref, tail):
    ts = u_ref.shape[1]

    @pl.when(pl.program_id(2) == 0)
    def _():
        tail[...] = jnp.zeros_like(tail)

    u = u_ref[0]
    row = lax.broadcasted_iota(jnp.int32, u.shape, 0)
    prev = tail[...]
    u1 = jnp.where(row == 0, prev[1:2], pltpu.roll(u, 1, axis=0))
    u2 = jnp.where(row == 0, prev[0:1], jnp.where(row == 1, prev[1:2], pltpu.roll(u, 2, axis=0)))
    tail[...] = u[ts - 2:ts, :]
    cw = cw_ref[...]
    conv = u2 * cw[0:1] + u1 * cw[1:2] + u * cw[2:3] + cb_ref[...]
    act = 0.5 * conv * (1.0 + lax.erf(conv * (2.0 ** -0.5)))
    o_ref[0] = (act * v_ref[0]).astype(o_ref.dtype)


def _conv_glu_act(uv, conv_w, conv_b):
    b, s, f2 = uv.shape
    f = f2 // 2
    ts, tf = _tile(s, 512), _tile(f, 512)
    nf = f // tf
    return pl.pallas_call(
        _glu_body, grid=(b, nf, s // ts),
        in_specs=[pl.BlockSpec((1, ts, tf), lambda i, c, j: (i, j, c)),
                  pl.BlockSpec((1, ts, tf), lambda i, c, j: (i, j, c + nf)),
                  pl.BlockSpec((conv_w.shape[0], tf), lambda i, c, j: (0, c)),
                  pl.BlockSpec((1, tf), lambda i, c, j: (0, c))],
        out_specs=pl.BlockSpec((1, ts, tf), lambda i, c, j: (i, j, c)),
        out_shape=jax.ShapeDtypeStruct((b, s, f), BF16),
        scratch_shapes=[pltpu.VMEM((2, tf), F32)],
        compiler_params=_cparams(("parallel", "parallel", "arbitrary")),
        name="conv_glu_act",
    )(uv, uv, conv_w, conv_b.reshape(1, f))


def _pad_cols(w, n):
    return jnp.pad(w, ((0, 0), (0, 0), (0, n - w.shape[2])))


def kernel(x, c, w_ada, b_ada, ada_table, norm_mix, w_in, a_mu, a_w0, a_w2, a_a0, a_a2, a_g2,
           a_kk_scale, a_ka, a_rk, a_gn_w, a_gn_b, b_fbias, merge_bias, w_br_a, w_br_b, w_br_c,
           w_out, norm_ffn, w_up, conv_w, conv_b, w_down, norm_final):
    bsz, seq, d = x.shape
    depth = ada_table.shape[0]
    m = bsz * seq
    aw = a_w0.shape[1]
    lora = a_w2.shape[1] + a_a2.shape[1] + a_g2.shape[1]
    bw, cw = w_br_b.shape[1], w_br_c.shape[1]
    b_heads, c_heads = bw // ATTN_HEAD_DIM, cw // ATTN_HEAD_DIM
    a_cols = 3 * aw + lora
    b_cols = 3 * bw + b_heads
    lora_pad = -(-lora // LANE) * LANE

    w_rkv = w_in[:, :, :3 * aw].astype(BF16)
    w_small = jnp.concatenate(
        [_pad_cols(w_in[:, :, 3 * aw:a_cols], lora_pad),
         _pad_cols(w_in[:, :, a_cols + 3 * bw:a_cols + b_cols], LANE)], axis=2).astype(BF16)
    w_bqkv = w_in[:, :, a_cols:a_cols + 3 * bw].astype(BF16)
    w_cqkv = w_in[:, :, a_cols + b_cols:a_cols + b_cols + 3 * cw].astype(BF16)
    w_gate = w_in[:, :, a_cols + b_cols + 3 * cw:].astype(BF16)
    w_br_a, w_br_b, w_br_c = w_br_a.astype(BF16), w_br_b.astype(BF16), w_br_c.astype(BF16)
    w_out, w_up, w_down = w_out.astype(BF16), w_up.astype(BF16), w_down.astype(BF16)
    mu_rkv = a_mu[:, :3 * aw]
    mu_small = jnp.pad(a_mu[:, 3 * aw:], ((0, 0), (0, lora_pad + LANE - lora)))
    fbias = jnp.pad(b_fbias, ((0, 0), (0, LANE - b_heads)))

    mods = _adaln(c, w_ada, b_ada, ada_table).reshape(depth, bsz, -1, 1, d)

    for l in range(depth):
        shift_m, scale_m, gate_m = mods[l, :, 0], mods[l, :, 1], mods[l, :, 2]
        shift_f, scale_f, gate_f = mods[l, :, 3], mods[l, :, 4], mods[l, :, 5]

        h = _rms_norm(x, norm_mix[l], scale_m, shift_m).reshape(m, d)
        rkv = _matmul(h, w_rkv, l, F32, "proj_rkv").reshape(bsz, seq, -1)
        small = _matmul(h, w_small, l, F32, "proj_small").reshape(bsz, seq, -1)
        bqkv = _matmul(h, w_bqkv, l, BF16, "proj_fox", tn_pref=768).reshape(bsz, seq, -1)
        cqkv = _matmul(h, w_cqkv, l, BF16, "proj_sb", tn_pref=768).reshape(bsz, seq, -1)
        pg = _matmul(h, w_gate, l, F32, "proj_gate")

        r, lw, k, v, a, b, gate = _rwkv_prep(
            rkv, small, mu_rkv[l], mu_small[l], a_w0[l], a_w2[l], a_a0[l], a_a2[l], a_g2[l],
            a_kk_scale[l].reshape(-1), a_ka[l].reshape(-1))
        y_a = _rwkv_recurrence(r, lw, k, v, a, b, gate, a_rk[l].reshape(-1), a_gn_w[l], a_gn_b[l])

        cum, cum_t = _fox_cum(small, lora_pad // LANE, fbias[l].reshape(1, LANE))
        y_b = _fox_attention(bqkv, cum, cum_t, b_heads)
        y_c = _sb_attention(cqkv, c_heads)

        merged = _merge(y_a.reshape(m, aw), y_b.reshape(m, bw), y_c.reshape(m, cw),
                        w_br_a, w_br_b, w_br_c, l, pg, merge_bias.reshape(depth, 3, 1, d))
        x = _matmul_residual(merged, w_out, l, x.reshape(m, d), gate_m, seq,
                             "out_proj").reshape(bsz, seq, d)

        h = _rms_norm(x, norm_ffn[l], scale_f, shift_f).reshape(m, d)
        uv = _matmul(h, w_up, l, F32, "ffn_up").reshape(bsz, seq, -1)
        act = _conv_glu_act(uv, conv_w[l], conv_b[l]).reshape(m, -1)
        x = _matmul_residual(act, w_down, l, x.reshape(m, d), gate_f, seq,
                             "ffn_down").reshape(bsz, seq, d)

    return _rms_norm(x, norm_final, out_dtype=x.dtype)
```

```python
import functools

import jax
import jax.numpy as jnp
from jax import lax
from jax.experimental import pallas as pl
from jax.experimental.pallas import tpu as pltpu

NORM_EPS = 1e-6
A_HEAD_DIM = 64
A_GN_EPS = 64e-5
ATTN_HEAD_DIM = 128
ATTN_Q_TILE = 256
SB_HEAD_GROUP = 4
FOX_HEAD_GROUP = 2
RWKV_CHUNK = 64
RWKV_PASSES = (1, 1, 1, 1)

LANE = 128
V7X_VMEM_LIMIT = 56 * 1024 * 1024

F32 = jnp.float32
BF16 = jnp.bfloat16


def _cparams(sem, vmem=None):
    return pltpu.CompilerParams(dimension_semantics=sem, vmem_limit_bytes=vmem)


def _tile(n, pref):
    if n <= pref:
        return n
    t = (pref // LANE) * LANE
    while n % t:
        t -= LANE
    return t


def _split(x, n):
    parts = []
    for _ in range(n - 1):
        p = x.astype(BF16)
        parts.append(p)
        x = x - p.astype(F32)
    parts.append(x.astype(BF16))
    return parts


def _dot(a, b):
    return jnp.dot(a, b, preferred_element_type=F32)


def _dot_nt(a, b):
    return lax.dot_general(a, b, (((1,), (1,)), ((), ())), preferred_element_type=F32)


def _dot_tn(a, b):
    return lax.dot_general(a, b, (((0,), (0,)), ((), ())), preferred_element_type=F32)


def _dot_x(a, b, passes, dot=_dot):
    if passes == 1:
        return dot(a.astype(BF16), b.astype(BF16))
    a1, a2 = _split(a, 2)
    b1, b2 = _split(b, 2)
    return dot(a1, b1) + (dot(a1, b2) + dot(a2, b1))


def _dot_exact_rhs(a, b01, n=3):
    parts = _split(a, n)
    acc = _dot(parts[0], b01)
    for p in parts[1:]:
        acc = acc + _dot(p, b01)
    return acc


def _dot_exact_lhs(a01, b, n=3):
    parts = _split(b, n)
    acc = _dot(a01, parts[0])
    for p in parts[1:]:
        acc = acc + _dot(a01, p)
    return acc


def _adaln_body(c_ref, w_ref, b_ref, t_ref, o_ref):
    c = c_ref[...]
    sc = c * jax.nn.sigmoid(c)
    base = _dot_x(sc, w_ref[...], 3) + b_ref[...]
    o_ref[...] = base[None, :, :] + t_ref[...][:, None, :]


def _adaln(c, w_ada, b_ada, ada_table):
    b, d = c.shape
    n = w_ada.shape[1]
    depth = ada_table.shape[0]
    tn = _tile(n, 512)
    return pl.pallas_call(
        _adaln_body,
        grid=(n // tn,),
        in_specs=[pl.BlockSpec((b, d), lambda j: (0, 0)),
                  pl.BlockSpec((d, tn), lambda j: (0, j)),
                  pl.BlockSpec((1, tn), lambda j: (0, j)),
                  pl.BlockSpec((depth, tn), lambda j: (0, j))],
        out_specs=pl.BlockSpec((depth, b, tn), lambda j: (0, 0, j)),
        out_shape=jax.ShapeDtypeStruct((depth, b, n), F32),
        compiler_params=_cparams(("parallel",), V7X_VMEM_LIMIT),
        name="adaln",
    )(c, w_ada, b_ada.reshape(1, n), ada_table.reshape(depth, n))


def _norm_mod_body(x_ref, g_ref, sc_ref, sh_ref, o_ref):
    x = x_ref[0]
    y = x * lax.rsqrt(jnp.mean(x * x, axis=-1, keepdims=True) + NORM_EPS) * g_ref[...]
    o_ref[0] = (y * (1.0 + sc_ref[0]) + sh_ref[0]).astype(o_ref.dtype)


def _norm_body(x_ref, g_ref, o_ref):
    x = x_ref[0]
    y = x * lax.rsqrt(jnp.mean(x * x, axis=-1, keepdims=True) + NORM_EPS) * g_ref[...]
    o_ref[0] = y.astype(o_ref.dtype)


def _rms_norm(x, gain, scale=None, shift=None, out_dtype=BF16):
    b, s, d = x.shape
    ts = _tile(s, 256)
    xspec = pl.BlockSpec((1, ts, d), lambda i, j: (i, j, 0))
    gspec = pl.BlockSpec((1, d), lambda i, j: (0, 0))
    mspec = pl.BlockSpec((1, 1, d), lambda i, j: (i, 0, 0))
    if scale is None:
        body, specs, args = _norm_body, [xspec, gspec], (x, gain.reshape(1, d))
    else:
        body, specs = _norm_mod_body, [xspec, gspec, mspec, mspec]
        args = (x, gain.reshape(1, d), scale, shift)
    return pl.pallas_call(
        body, grid=(b, s // ts), in_specs=specs, out_specs=xspec,
        out_shape=jax.ShapeDtypeStruct((b, s, d), out_dtype),
        compiler_params=_cparams(("parallel", "parallel")),
        name="rms_norm",
    )(*args)


def _mm_body(x_ref, w_ref, o_ref):
    o_ref[...] = _dot(x_ref[...], w_ref[...]).astype(o_ref.dtype)


def _wspec(w, layer, tn):
    return pl.BlockSpec((None, w.shape[1], tn), lambda i, j: (layer, 0, j))


def _matmul(x, w, layer, out_dtype, name, tm_pref=1024, tn_pref=1024):
    m, k = x.shape
    n = w.shape[2]
    tm, tn = _tile(m, tm_pref), _tile(n, tn_pref)
    return pl.pallas_call(
        _mm_body, grid=(m // tm, n // tn),
        in_specs=[pl.BlockSpec((tm, k), lambda i, j: (i, 0)), _wspec(w, layer, tn)],
        out_specs=pl.BlockSpec((tm, tn), lambda i, j: (i, j)),
        out_shape=jax.ShapeDtypeStruct((m, n), out_dtype),
        compiler_params=_cparams(("parallel", "parallel"), V7X_VMEM_LIMIT),
        name=name,
    )(x, w)


def _mm_res_body(x_ref, w_ref, r_ref, g_ref, o_ref):
    o_ref[...] = r_ref[...] + g_ref[0] * _dot(x_ref[...], w_ref[...])


def _matmul_residual(x, w, layer, res, gate, seq, name, tm_pref=1024, tn_pref=512):
    m, k = x.shape
    n = w.shape[2]
    tm, tn = _tile(seq, tm_pref), _tile(n, tn_pref)
    per_batch = seq // tm
    return pl.pallas_call(
        _mm_res_body, grid=(m // tm, n // tn),
        in_specs=[pl.BlockSpec((tm, k), lambda i, j: (i, 0)),
                  _wspec(w, layer, tn),
                  pl.BlockSpec((tm, tn), lambda i, j: (i, j)),
                  pl.BlockSpec((1, 1, tn), lambda i, j: (i // per_batch, 0, j))],
        out_specs=pl.BlockSpec((tm, tn), lambda i, j: (i, j)),
        out_shape=jax.ShapeDtypeStruct((m, n), F32),
        compiler_params=_cparams(("parallel", "parallel"), V7X_VMEM_LIMIT),
        name=name,
    )(x, w, res, gate)


def _merge_body(ya_ref, yb_ref, yc_ref, wa_ref, wb_ref, wc_ref,
                ga_ref, gb_ref, gc_ref, bias_ref, o_ref):
    bias = bias_ref[...]
    acc = jax.nn.sigmoid(ga_ref[...] + bias[0]) * _dot(ya_ref[...], wa_ref[...])
    acc = acc + jax.nn.sigmoid(gb_ref[...] + bias[1]) * _dot(yb_ref[...], wb_ref[...])
    acc = acc + jax.nn.sigmoid(gc_ref[...] + bias[2]) * _dot(yc_ref[...], wc_ref[...])
    o_ref[...] = acc.astype(o_ref.dtype)


def _merge(ya, yb, yc, wa, wb, wc, layer, pg, bias, tm_pref=1024, tn_pref=512):
    m = ya.shape[0]
    d = wa.shape[2]
    tm, tn = _tile(m, tm_pref), _tile(d, tn_pref)
    nb = d // tn

    def yspec(y):
        return pl.BlockSpec((tm, y.shape[1]), lambda i, j: (i, 0))

    def gspec(k):
        return pl.BlockSpec((tm, tn), lambda i, j: (i, j + k * nb))

    return pl.pallas_call(
        _merge_body, grid=(m // tm, nb),
        in_specs=[yspec(ya), yspec(yb), yspec(yc),
                  _wspec(wa, layer, tn), _wspec(wb, layer, tn), _wspec(wc, layer, tn),
                  gspec(0), gspec(1), gspec(2),
                  pl.BlockSpec((None, 3, 1, tn), lambda i, j: (layer, 0, 0, j))],
        out_specs=pl.BlockSpec((tm, tn), lambda i, j: (i, j)),
        out_shape=jax.ShapeDtypeStruct((m, d), BF16),
        compiler_params=_cparams(("parallel", "parallel"), V7X_VMEM_LIMIT),
        name="merge",
    )(ya, yb, yc, wa, wb, wc, pg, pg, pg, bias)


def _head_ones(width):
    r = lax.broadcasted_iota(jnp.int32, (width, width), 0) // A_HEAD_DIM
    c = lax.broadcasted_iota(jnp.int32, (width, width), 1) // A_HEAD_DIM
    return (r == c).astype(BF16)


def _head_sum(x, ones):
    pieces = [_dot_exact_rhs(x[:, p:p + LANE], ones) for p in range(0, x.shape[1], LANE)]
    return pieces[0] if len(pieces) == 1 else jnp.concatenate(pieces, axis=1)


def _rwkv_prep_body(dims, rkv_ref, sm_ref, mu_rkv_ref, mu_sm_ref, w0_ref, w2_ref, a0_ref,
                    a2_ref, g2_ref, kks_ref, ka_ref,
                    r_ref, lw_ref, k_ref, v_ref, a_ref, b_ref, gate_ref,
                    prev_rkv, prev_sm):
    w, lora_pad = dims
    ts = rkv_ref.shape[1]

    @pl.when(pl.program_id(1) == 0)
    def _():
        prev_rkv[...] = jnp.zeros_like(prev_rkv)
        prev_sm[...] = jnp.zeros_like(prev_sm)

    def shifted(cur, prev_ref):
        rolled = pltpu.roll(cur, 1, axis=0)
        row = lax.broadcasted_iota(jnp.int32, cur.shape, 0)
        prev = jnp.where(row == 0, prev_ref[...], rolled)
        prev_ref[...] = cur[ts - 1:ts, :]
        return prev

    rkv = rkv_ref[0]
    sm = sm_ref[0]
    z = rkv + (shifted(rkv, prev_rkv) - rkv) * mu_rkv_ref[...]
    zs = sm + (shifted(sm, prev_sm) - sm) * mu_sm_ref[...]
    r, k, v = z[:, :w], z[:, w:2 * w], z[:, 2 * w:3 * w]
    zl = zs[:, :w2_ref.shape[0]]
    zg = zs[:, lora_pad - g2_ref.shape[0]:lora_pad]
    log_w = -jax.nn.softplus(-(w0_ref[...] + _dot_x(jnp.tanh(zl), w2_ref[...], 1))) - 0.5
    iclr = jax.nn.sigmoid(a0_ref[...] + _dot_x(zl, a2_ref[...], 1))
    gate = _dot_x(jax.nn.sigmoid(zg), g2_ref[...], 1)

    ones = _head_ones(LANE)
    kk = k * kks_ref[...]
    norm = jnp.sqrt(_head_sum(kk * kk, ones))
    kk = kk / jnp.maximum(norm, 1e-12)

    r_ref[0] = r
    lw_ref[0] = -jnp.exp(log_w)
    k_ref[0] = k * (1.0 + (iclr - 1.0) * ka_ref[...])
    v_ref[0] = v
    a_ref[0] = -kk
    b_ref[0] = kk * iclr
    gate_ref[0] = gate


def _rwkv_prep(rkv, small, mu_rkv, mu_small, w0, w2, a0, a2, g2, kk_scale, k_a):
    b, s, w3 = rkv.shape
    w = w3 // 3
    ns = small.shape[2]
    dl, il, gl = w2.shape[0], a2.shape[0], g2.shape[0]
    lora_pad = -(-(dl + il + gl) // LANE) * LANE
    k1 = -(-(dl + il) // LANE) * LANE
    g0 = (dl + il) // LANE * LANE
    w2 = jnp.pad(w2, ((0, k1 - dl), (0, 0)))
    a2 = jnp.pad(a2, ((dl, k1 - dl - il), (0, 0)))
    g2 = jnp.pad(g2, ((dl + il - g0, lora_pad - dl - il - gl), (0, 0)))
    ts = _tile(s, 256)
    row = lambda arr: arr.reshape(1, -1)
    full = lambda arr: pl.BlockSpec(arr.shape, lambda i, j: (0,) * arr.ndim)
    params = [row(mu_rkv), row(mu_small), row(w0), w2, row(a0), a2, g2, row(kk_scale), row(k_a)]
    out_spec = pl.BlockSpec((1, ts, w), lambda i, j: (i, j, 0))
    out = jax.ShapeDtypeStruct((b, s, w), F32)
    return pl.pallas_call(
        functools.partial(_rwkv_prep_body, (w, lora_pad)),
        grid=(b, s // ts),
        in_specs=[pl.BlockSpec((1, ts, w3), lambda i, j: (i, j, 0)),
                  pl.BlockSpec((1, ts, ns), lambda i, j: (i, j, 0))] + [full(p) for p in params],
        out_specs=[out_spec] * 7,
        out_shape=[out] * 7,
        scratch_shapes=[pltpu.VMEM((1, w3), F32), pltpu.VMEM((1, ns), F32)],
        compiler_params=_cparams(("parallel", "arbitrary"), V7X_VMEM_LIMIT),
        name="rwkv_prep",
    )(rkv, small, *params)


def _rwkv_rec_body(r_ref, lw_ref, k_ref, v_ref, a_ref, b_ref, gate_ref, rk_ref, gnw_ref,
                   gnb_ref, o_ref, state):
    c = RWKV_CHUNK
    n_pairs = r_ref.shape[2] // LANE

    @pl.when(pl.program_id(1) == 0)
    def _():
        state[...] = jnp.zeros_like(state)

    row = lax.broadcasted_iota(jnp.int32, (c, c), 0)
    col = lax.broadcasted_iota(jnp.int32, (c, c), 1)
    tri_incl = (row >= col).astype(BF16)

    r2 = lax.broadcasted_iota(jnp.int32, (2 * c, LANE), 0)
    c2 = lax.broadcasted_iota(jnp.int32, (2 * c, LANE), 1)
    same_head = (r2 // c) == (c2 // A_HEAD_DIM)
    t_row, t_col = r2 % c, c2 % c
    strict = same_head & (t_row > t_col)
    incl = same_head & (t_row >= t_col)
    eye = (r2 == c2).astype(F32)
    ones = _head_ones(LANE)

    def bd(x):
        return jnp.where(same_head, jnp.concatenate([x, x], axis=0), 0.0)

    cat0 = lambda *xs: jnp.concatenate(xs, axis=0)
    cat1 = lambda *xs: jnp.concatenate(xs, axis=1)
    hh = 2 * c
    pairs = range(n_pairs)
    sls = [slice(p * LANE, (p + 1) * LANE) for p in pairs]
    each = lambda fn, *cols: [fn(*args) for args in zip(*cols)]
    load = lambda ref: [ref[0, :, sl] for sl in sls]
    r, lw, k, v, a, b = (load(ref) for ref in (r_ref, lw_ref, k_ref, v_ref, a_ref, b_ref))

    cl = each(lambda x: _dot_exact_lhs(tri_incl, x), lw)
    cl_last = each(lambda x: x[c - 1:c, :], cl)
    e_neg = each(lambda x: jnp.exp(-x), cl)
    xa = each(lambda a_, cl_, lw_: bd(a_ * jnp.exp(cl_ - lw_)), a, cl, lw)
    xr = each(lambda r_, cl_: bd(r_ * jnp.exp(cl_)), r, cl)
    yb = each(lambda b_, e_: bd(b_ * e_), b, e_neg)
    yk = each(lambda k_, e_: bd(k_ * e_), k, e_neg)
    e_end = each(lambda last, cl_: jnp.exp(last - cl_), cl_last, cl)
    bh = each(lambda b_, e_: bd(b_ * e_), b, e_end)
    kh = each(lambda k_, e_: bd(k_ * e_), k, e_end)
    vv = each(bd, v)
    w_end = each(jnp.exp, cl_last)

    amat = each(lambda xa_, xr_, yb_, yk_: _dot_x(cat0(xa_, xr_), cat0(yb_, yk_),
                                                  RWKV_PASSES[0], _dot_nt), xa, xr, yb, yk)
    a_ab = each(lambda m_: jnp.where(strict, m_[:hh, :hh], 0.0), amat)
    a_ak = each(lambda m_: jnp.where(strict, m_[:hh, hh:], 0.0), amat)
    a_rb = each(lambda m_: jnp.where(incl, m_[hh:, :hh], 0.0), amat)
    a_rk = each(lambda m_: jnp.where(incl, m_[hh:, hh:], 0.0), amat)

    pt = RWKV_PASSES[1]
    t_inv = each(lambda x: eye + x, a_ab)
    pw = each(lambda x: _dot_x(x, x, pt), a_ab)
    n = 2
    while 2 * n < c:
        both = each(lambda pw_, t_: _dot_x(pw_, cat1(pw_, t_), pt), pw, t_inv)
        pw = each(lambda x: x[:, :hh], both)
        t_inv = each(lambda t_, x: t_ + x[:, hh:], t_inv, both)
        n *= 2
    t_inv = each(lambda pw_, t_: t_ + _dot_x(pw_, t_, pt), pw, t_inv)

    po = RWKV_PASSES[2]
    akv = each(lambda x, y: _dot_x(x, y, po), a_ak, vv)
    pq = each(lambda t_, xa_, akv_: _dot_x(t_, cat1(xa_, akv_), po), t_inv, xa, akv)
    rhs = each(lambda pq_, vv_: cat0(pq_, cat1(jnp.zeros_like(vv_), vv_)), pq, vv)
    ytop = each(lambda x, y, rhs_: _dot_x(cat1(x, y), rhs_, po), a_rb, a_rk, rhs)
    sbot = each(lambda x, y, rhs_: _dot_x(cat0(x, y), rhs_, po, _dot_tn), bh, kh, rhs)
    g = each(lambda xr_, t: xr_ + t[:, :hh], xr, ytop)
    m = each(lambda w, t: eye * w + t[:, :hh], w_end, sbot)

    gm = [_dot_x(cat0(g[p], m[p]), state[p], RWKV_PASSES[3]) for p in pairs]
    for p in pairs:
        state[p] = gm[p][hh:] + sbot[p][:, hh:]
    y_bd = each(lambda gm_, t: gm_[:hh] + t[:, hh:], gm, ytop)
    y = each(lambda x: x[:c] + x[c:], y_bd)

    mean = each(lambda x: _head_sum(x, ones) * (1.0 / A_HEAD_DIM), y)
    yc = each(lambda x, mu: x - mu, y, mean)
    var = each(lambda x: _head_sum(x * x, ones) * (1.0 / A_HEAD_DIM), yc)
    bonus = [_head_sum(r[p] * k[p] * rk_ref[:, sls[p]], ones) * v[p] for p in pairs]
    outs = [(yc[p] * lax.rsqrt(var[p] + A_GN_EPS) * gnw_ref[:, sls[p]] + gnb_ref[:, sls[p]]
             + bonus[p]) * gate_ref[0, :, sls[p]] for p in pairs]
    o_ref[0] = jnp.concatenate(outs, axis=1).astype(o_ref.dtype)


def _rwkv_recurrence(r, lw, k, v, a, b, gate, r_k, gn_w, gn_b):
    bsz, s, w = r.shape
    c = RWKV_CHUNK
    blk = pl.BlockSpec((1, c, w), lambda i, j: (i, j, 0))
    par = pl.BlockSpec((1, w), lambda i, j: (0, 0))
    return pl.pallas_call(
        _rwkv_rec_body, grid=(bsz, s // c),
        in_specs=[blk] * 7 + [par] * 3,
        out_specs=blk,
        out_shape=jax.ShapeDtypeStruct((bsz, s, w), BF16),
        scratch_shapes=[pltpu.VMEM((w // LANE, LANE, LANE), F32)],
        compiler_params=_cparams(("parallel", "arbitrary")),
        name="rwkv_recurrence",
    )(r, lw, k, v, a, b, gate, r_k.reshape(1, w), gn_w.reshape(1, w), gn_b.reshape(1, w))


def _fox_cum_body(f_ref, bias_ref, cum_ref, cum_t_ref):
    s = f_ref.shape[1]
    t = _tile(s, 256)
    row = lax.broadcasted_iota(jnp.int32, (t, t), 0)
    col = lax.broadcasted_iota(jnp.int32, (t, t), 1)
    tri = (row >= col).astype(BF16)
    carry = jnp.zeros((1, LANE), F32)
    for i in range(s // t):
        ls = jax.nn.log_sigmoid(f_ref[0, i * t:(i + 1) * t, :] + bias_ref[...])
        cum = _dot_exact_lhs(tri, ls) + carry
        carry = cum[t - 1:t, :]
        cum_ref[0, i * t:(i + 1) * t, :] = cum
        cum_t_ref[0, :, i * t:(i + 1) * t] = cum.T


def _fox_cum(small, lane_block, bias):
    b, s, _ = small.shape
    return pl.pallas_call(
        _fox_cum_body, grid=(b,),
        in_specs=[pl.BlockSpec((1, s, LANE), lambda i: (i, 0, lane_block)),
                  pl.BlockSpec((1, LANE), lambda i: (0, 0))],
        out_specs=[pl.BlockSpec((1, s, LANE), lambda i: (i, 0, 0)),
                   pl.BlockSpec((1, LANE, s), lambda i: (i, 0, 0))],
        out_shape=[jax.ShapeDtypeStruct((b, s, LANE), F32),
                   jax.ShapeDtypeStruct((b, LANE, s), F32)],
        compiler_params=_cparams(("parallel",)),
        name="fox_cum",
    )(small, bias)


def _fox_body(tq, q_ref, k_ref, v_ref, cum_ref, cum_t_ref, o_ref):
    s = q_ref.shape[1]
    dh = ATTN_HEAD_DIM
    group = q_ref.shape[2] // dh
    heads = [slice(i * dh, (i + 1) * dh) for i in range(group)]
    each = lambda fn, *cols: [fn(*args) for args in zip(*cols)]
    scale = dh ** -0.5
    lane_row = lax.broadcasted_iota(jnp.int32, (LANE, LANE), 0)
    sel = [(lane_row == pl.program_id(1) * group + i).astype(BF16) for i in range(group)]
    row = lax.broadcasted_iota(jnp.int32, (tq, tq), 0)
    col = lax.broadcasted_iota(jnp.int32, (tq, tq), 1)
    causal = row >= col
    rowmax = lambda x: jnp.max(x, axis=1, keepdims=True)
    rowsum = lambda x: jnp.sum(x, axis=1, keepdims=True)

    for qi in range(s // tq):
        q0, q1 = qi * tq, (qi + 1) * tq
        q = [q_ref[0, q0:q1, h] for h in heads]
        cum_q = [_dot_exact_rhs(cum_ref[0, q0:q1, :], sel_) for sel_ in sel]

        def logits(k0, k1):
            return [_dot_nt(q[i], k_ref[0, k0:k1, heads[i]]) * scale
                    + jnp.concatenate([cum_q[i]] * ((k1 - k0) // LANE), axis=1)
                    - cum_t_ref[0, i, :, k0:k1] for i in range(group)]

        s_diag = each(lambda x: jnp.where(causal, x, -jnp.inf), logits(q0, q1))
        m = each(rowmax, s_diag)
        if qi:
            s_past = logits(0, q0)
            m = each(lambda m_, x: jnp.maximum(m_, rowmax(x)), m, s_past)
        p = each(lambda x, m_: jnp.exp(x - m_), s_diag, m)
        l = each(rowsum, p)
        acc = [_dot(p_.astype(BF16), v_ref[0, q0:q1, h]) for p_, h in zip(p, heads)]
        if qi:
            p = each(lambda x, m_: jnp.exp(x - m_), s_past, m)
            l = each(lambda l_, p_: l_ + rowsum(p_), l, p)
            acc = [a_ + _dot(p_.astype(BF16), v_ref[0, :q0, h])
                   for a_, p_, h in zip(acc, p, heads)]
        for a_, l_, h in zip(acc, l, heads):
            o_ref[0, q0:q1, h] = (a_ / l_).astype(o_ref.dtype)


def _fox_attention(qkv, cum, cum_t, n_heads):
    b, s, _ = qkv.shape
    tq = _tile(s, ATTN_Q_TILE)
    group = FOX_HEAD_GROUP if n_heads % FOX_HEAD_GROUP == 0 else 1
    ng = n_heads // group
    blk = lambda off: pl.BlockSpec((1, s, group * ATTN_HEAD_DIM), lambda i, h: (i, 0, off + h))
    return pl.pallas_call(
        functools.partial(_fox_body, tq), grid=(b, ng),
        in_specs=[blk(0), blk(ng), blk(2 * ng),
                  pl.BlockSpec((1, s, LANE), lambda i, h: (i, 0, 0)),
                  pl.BlockSpec((1, group, 1, s), lambda i, h: (i, h, 0, 0))],
        out_specs=blk(0),
        out_shape=jax.ShapeDtypeStruct((b, s, n_heads * ATTN_HEAD_DIM), BF16),
        compiler_params=_cparams(("parallel", "parallel"), V7X_VMEM_LIMIT),
        name="fox_attention",
    )(qkv, qkv, qkv, cum, cum_t.reshape(b, LANE, 1, s))


def _log_sigmoid(z):
    return jnp.minimum(z, 0.0) - jnp.log(1.0 + jnp.exp(-jnp.abs(z)))


def _sb_body(t, q_ref, k_ref, v_ref, o_ref):
    s = q_ref.shape[1]
    dh = ATTN_HEAD_DIM
    heads = [slice(i * dh, (i + 1) * dh) for i in range(q_ref.shape[2] // dh)]
    each = lambda fn, *cols: [fn(*args) for args in zip(*cols)]
    scale = dh ** -0.5
    row = lax.broadcasted_iota(jnp.int32, (t, t), 0)
    col = lax.broadcasted_iota(jnp.int32, (t, t), 1)
    strict = row > col
    after = strict.astype(BF16)

    for qi in range(s // t):
        rows = slice(qi * t, (qi + 1) * t)
        q = [q_ref[0, rows, h] for h in heads]
        later_rows = [jnp.zeros((t, 1), F32) for _ in heads]
        acc = [jnp.zeros((t, dh), F32) for _ in heads]
        for kj in range(qi, -1, -1):
            keys = slice(kj * t, (kj + 1) * t)
            z = [_dot_nt(q_, k_ref[0, keys, h]) * scale for q_, h in zip(q, heads)]
            log_take = each(_log_sigmoid, z)
            log_stay = each(lambda lt, z_: lt - z_, log_take, z)
            if kj == qi:
                log_stay = each(lambda x: jnp.where(strict, x, 0.0), log_stay)
            later = each(lambda lr, ls: lr + _dot_exact_rhs(ls, after, 2), later_rows, log_stay)
            wts = each(lambda lt, la: jnp.exp(lt + la), log_take, later)
            if kj == qi:
                wts = each(lambda x: jnp.where(strict, x, 0.0), wts)
            acc = [a_ + _dot(w_.astype(BF16), v_ref[0, keys, h])
                   for a_, w_, h in zip(acc, wts, heads)]
            if kj:
                later_rows = each(lambda lr, ls: lr + jnp.sum(ls, axis=1, keepdims=True),
                                  later_rows, log_stay)
        for a_, h in zip(acc, heads):
            o_ref[0, rows, h] = a_.astype(o_ref.dtype)


def _sb_attention(qkv, n_heads):
    b, s, _ = qkv.shape
    t = _tile(s, ATTN_Q_TILE)
    group = SB_HEAD_GROUP if n_heads % SB_HEAD_GROUP == 0 else 1
    ng = n_heads // group
    blk = lambda off: pl.BlockSpec((1, s, group * ATTN_HEAD_DIM), lambda i, h: (i, 0, off + h))
    return pl.pallas_call(
        functools.partial(_sb_body, t), grid=(b, ng),
        in_specs=[blk(0), blk(ng), blk(2 * ng)],
        out_specs=blk(0),
        out_shape=jax.ShapeDtypeStruct((b, s, n_heads * ATTN_HEAD_DIM), BF16),
        compiler_params=_cparams(("parallel", "parallel"), V7X_VMEM_LIMIT),
        name="sb_attention",
    )(qkv, qkv, qkv)


def _ffn_up_body(per_seq, rows, x_ref, wu_ref, wv_ref, cw_ref, cb_ref, o_ref, tail):
    tm = x_ref.shape[0]

    @pl.when(pl.program_id(1) % per_seq == 0)
    def _():
        tail[...] = jnp.zeros_like(tail)

    cw = cw_ref[...]
    cb = cb_ref[...]
    wu, wv = wu_ref[...], wv_ref[...]
    row = lax.broadcasted_iota(jnp.int32, (rows, wu.shape[1]), 0)
    project = lambda r: (_dot(x_ref[r * rows:(r + 1) * rows, :], wu),
                         _dot(x_ref[r * rows:(r + 1) * rows, :], wv))
    prev = tail[...]
    u, v = project(0)
    for r in range(tm // rows):
        nxt = project(r + 1) if (r + 1) * rows < tm else None
        u1 = jnp.where(row == 0, prev[1:2], pltpu.roll(u, 1, axis=0))
        u2 = jnp.where(row == 0, prev[0:1],
                       jnp.where(row == 1, prev[1:2], pltpu.roll(u, 2, axis=0)))
        conv = u2 * cw[0:1] + u1 * cw[1:2] + u * cw[2:3] + cb
        act = 0.5 * conv * (1.0 + lax.erf(conv * (2.0 ** -0.5)))
        o_ref[r * rows:(r + 1) * rows, :] = (act * v).astype(o_ref.dtype)
        prev = u[rows - 2:rows, :]
        if nxt is not None:
            u, v = nxt
    tail[...] = prev


def _ffn_up_glu(x, w_up, layer, conv_w, conv_b, seq, tm_pref=1024, tn_pref=512):
    m, k = x.shape
    f = w_up.shape[2] // 2
    tm, tn = _tile(seq, tm_pref), _tile(f, tn_pref)
    nf = f // tn
    rows = _tile(tm, 256)
    wspec = lambda off: pl.BlockSpec((None, k, tn), lambda j, i: (layer, 0, j + off))
    return pl.pallas_call(
        functools.partial(_ffn_up_body, seq // tm, rows), grid=(nf, m // tm),
        in_specs=[pl.BlockSpec((tm, k), lambda j, i: (i, 0)), wspec(0), wspec(nf),
                  pl.BlockSpec((conv_w.shape[0], tn), lambda j, i: (0, j)),
                  pl.BlockSpec((1, tn), lambda j, i: (0, j))],
        out_specs=pl.BlockSpec((tm, tn), lambda j, i: (i, j)),
        out_shape=jax.ShapeDtypeStruct((m, f), BF16),
        scratch_shapes=[pltpu.VMEM((2, tn), F32)],
        compiler_params=_cparams(("parallel", "arbitrary"), V7X_VMEM_LIMIT),
        name="ffn_up_glu",
    )(x, w_up, w_up, conv_w, conv_b.reshape(1, f))


def _stage_body(groups, w_ref, *o_refs):
    for pieces, o_ref in zip(groups, o_refs):
        if len(pieces) > 1:
            o_ref[...] = jnp.zeros_like(o_ref)
        for src, dst, width in pieces:
            o_ref[0, :, dst:dst + width] = w_ref[0, :, src:src + width].astype(o_ref.dtype)


def _stage_w_in(w_in, groups, widths, tk=64):
    depth, k, n = w_in.shape
    return pl.pallas_call(
        functools.partial(_stage_body, groups), grid=(depth, k // tk),
        in_specs=[pl.BlockSpec((1, tk, n), lambda l, i: (l, i, 0))],
        out_specs=[pl.BlockSpec((1, tk, wd), lambda l, i: (l, i, 0)) for wd in widths],
        out_shape=[jax.ShapeDtypeStruct((depth, k, wd), BF16) for wd in widths],
        compiler_params=_cparams(("parallel", "parallel"), V7X_VMEM_LIMIT),
        name="stage_w_in",
    )(w_in)


def kernel(x, c, w_ada, b_ada, ada_table, norm_mix, w_in, a_mu, a_w0, a_w2, a_a0, a_a2, a_g2,
           a_kk_scale, a_ka, a_rk, a_gn_w, a_gn_b, b_fbias, merge_bias, w_br_a, w_br_b, w_br_c,
           w_out, norm_ffn, w_up, conv_w, conv_b, w_down, norm_final):
    bsz, seq, d = x.shape
    depth = ada_table.shape[0]
    m = bsz * seq
    aw = a_w0.shape[1]
    lora = a_w2.shape[1] + a_a2.shape[1] + a_g2.shape[1]
    bw, cw = w_br_b.shape[1], w_br_c.shape[1]
    b_heads, c_heads = bw // ATTN_HEAD_DIM, cw // ATTN_HEAD_DIM
    a_cols = 3 * aw + lora
    b_cols = 3 * bw + b_heads
    lora_pad = -(-lora // LANE) * LANE

    c_start = a_cols + b_cols
    g_start = c_start + 3 * cw
    w_rkv, w_small, w_bqkv, w_cqkv, w_gate = _stage_w_in(
        w_in,
        [[(0, 0, 3 * aw)],
         [(3 * aw, 0, lora), (a_cols + 3 * bw, lora_pad, b_heads)],
         [(a_cols, 0, 3 * bw)],
         [(c_start, 0, 3 * cw)],
         [(g_start, 0, w_in.shape[2] - g_start)]],
        [3 * aw, lora_pad + LANE, 3 * bw, 3 * cw, w_in.shape[2] - g_start])
    w_br_a, w_br_b, w_br_c = w_br_a.astype(BF16), w_br_b.astype(BF16), w_br_c.astype(BF16)
    w_out, w_up, w_down = w_out.astype(BF16), w_up.astype(BF16), w_down.astype(BF16)
    mu_rkv = a_mu[:, :3 * aw]
    mu_small = jnp.pad(a_mu[:, 3 * aw:], ((0, 0), (0, lora_pad + LANE - lora)))
    fbias = jnp.pad(b_fbias, ((0, 0), (0, LANE - b_heads)))

    mods = _adaln(c, w_ada, b_ada, ada_table).reshape(depth, bsz, -1, 1, d)

    for l in range(depth):
        shift_m, scale_m, gate_m = mods[l, :, 0], mods[l, :, 1], mods[l, :, 2]
        shift_f, scale_f, gate_f = mods[l, :, 3], mods[l, :, 4], mods[l, :, 5]

        h = _rms_norm(x, norm_mix[l], scale_m, shift_m).reshape(m, d)
        rkv = _matmul(h, w_rkv, l, F32, "proj_rkv").reshape(bsz, seq, -1)
        small = _matmul(h, w_small, l, F32, "proj_small").reshape(bsz, seq, -1)
        bqkv = _matmul(h, w_bqkv, l, BF16, "proj_fox", tn_pref=768).reshape(bsz, seq, -1)
        cqkv = _matmul(h, w_cqkv, l, BF16, "proj_sb", tn_pref=768).reshape(bsz, seq, -1)
        pg = _matmul(h, w_gate, l, F32, "proj_gate")

        r, lw, k, v, a, b, gate = _rwkv_prep(
            rkv, small, mu_rkv[l], mu_small[l], a_w0[l], a_w2[l], a_a0[l], a_a2[l], a_g2[l],
            a_kk_scale[l].reshape(-1), a_ka[l].reshape(-1))
        y_a = _rwkv_recurrence(r, lw, k, v, a, b, gate, a_rk[l].reshape(-1), a_gn_w[l], a_gn_b[l])

        cum, cum_t = _fox_cum(small, lora_pad // LANE, fbias[l].reshape(1, LANE))
        y_b = _fox_attention(bqkv, cum, cum_t, b_heads)
        y_c = _sb_attention(cqkv, c_heads)

        merged = _merge(y_a.reshape(m, aw), y_b.reshape(m, bw), y_c.reshape(m, cw),
                        w_br_a, w_br_b, w_br_c, l, pg, merge_bias.reshape(depth, 3, 1, d))
        x = _matmul_residual(merged, w_out, l, x.reshape(m, d), gate_m, seq,
                             "out_proj").reshape(bsz, seq, d)

        h = _rms_norm(x, norm_ffn[l], scale_f, shift_f).reshape(m, d)
        act = _ffn_up_glu(h, w_up, l, conv_w[l], conv_b[l], seq)
        x = _matmul_residual(act, w_down, l, x.reshape(m, d), gate_f, seq,
                             "ffn_down").reshape(bsz, seq, d)

    return _rms_norm(x, norm_final, out_dtype=x.dtype)
```

```python
import functools

import jax
import jax.numpy as jnp
from jax import lax
from jax.experimental import pallas as pl
from jax.experimental.pallas import tpu as pltpu

NORM_EPS = 1e-6
A_HEAD_DIM = 64
A_GN_EPS = 64e-5
ATTN_HEAD_DIM = 128
ATTN_Q_TILE = 256
SB_HEAD_GROUP = 4
FOX_HEAD_GROUP = 2
RWKV_CHUNK = 64
RWKV_PASSES = (1, 1, 1, 1)

LANE = 128
V7X_VMEM_LIMIT = 56 * 1024 * 1024

F32 = jnp.float32
BF16 = jnp.bfloat16


def _cparams(sem, vmem=None):
    return pltpu.CompilerParams(dimension_semantics=sem, vmem_limit_bytes=vmem)


def _tile(n, pref):
    if n <= pref:
        return n
    t = (pref // LANE) * LANE
    while n % t:
        t -= LANE
    return t


def _split(x, n):
    parts = []
    for _ in range(n - 1):
        p = x.astype(BF16)
        parts.append(p)
        x = x - p.astype(F32)
    parts.append(x.astype(BF16))
    return parts


def _dot(a, b):
    return jnp.dot(a, b, preferred_element_type=F32)


def _dot_nt(a, b):
    return lax.dot_general(a, b, (((1,), (1,)), ((), ())), preferred_element_type=F32)


def _dot_tn(a, b):
    return lax.dot_general(a, b, (((0,), (0,)), ((), ())), preferred_element_type=F32)


def _dot_x(a, b, passes, dot=_dot):
    if passes == 1:
        return dot(a.astype(BF16), b.astype(BF16))
    a1, a2 = _split(a, 2)
    b1, b2 = _split(b, 2)
    return dot(a1, b1) + (dot(a1, b2) + dot(a2, b1))


def _dot_exact_rhs(a, b01, n=3):
    parts = _split(a, n)
    acc = _dot(parts[0], b01)
    for p in parts[1:]:
        acc = acc + _dot(p, b01)
    return acc


def _dot_exact_lhs(a01, b, n=3):
    parts = _split(b, n)
    acc = _dot(a01, parts[0])
    for p in parts[1:]:
        acc = acc + _dot(a01, p)
    return acc


def _adaln_body(c_ref, w_ref, b_ref, t_ref, o_ref):
    c = c_ref[...]
    sc = c * jax.nn.sigmoid(c)
    base = _dot_x(sc, w_ref[...], 3) + b_ref[...]
    o_ref[...] = base[None, :, :] + t_ref[...][:, None, :]


def _adaln(c, w_ada, b_ada, ada_table):
    b, d = c.shape
    n = w_ada.shape[1]
    depth = ada_table.shape[0]
    tn = _tile(n, 512)
    return pl.pallas_call(
        _adaln_body,
        grid=(n // tn,),
        in_specs=[pl.BlockSpec((b, d), lambda j: (0, 0)),
                  pl.BlockSpec((d, tn), lambda j: (0, j)),
                  pl.BlockSpec((1, tn), lambda j: (0, j)),
                  pl.BlockSpec((depth, tn), lambda j: (0, j))],
        out_specs=pl.BlockSpec((depth, b, tn), lambda j: (0, 0, j)),
        out_shape=jax.ShapeDtypeStruct((depth, b, n), F32),
        compiler_params=_cparams(("parallel",), V7X_VMEM_LIMIT),
        name="adaln",
    )(c, w_ada, b_ada.reshape(1, n), ada_table.reshape(depth, n))


def _norm_mod_body(x_ref, g_ref, sc_ref, sh_ref, o_ref):
    x = x_ref[0]
    y = x * lax.rsqrt(jnp.mean(x * x, axis=-1, keepdims=True) + NORM_EPS) * g_ref[...]
    o_ref[0] = (y * (1.0 + sc_ref[0]) + sh_ref[0]).astype(o_ref.dtype)


def _norm_body(x_ref, g_ref, o_ref):
    x = x_ref[0]
    y = x * lax.rsqrt(jnp.mean(x * x, axis=-1, keepdims=True) + NORM_EPS) * g_ref[...]
    o_ref[0] = y.astype(o_ref.dtype)


def _rms_norm(x, gain, scale=None, shift=None, out_dtype=BF16):
    b, s, d = x.shape
    ts = _tile(s, 256)
    xspec = pl.BlockSpec((1, ts, d), lambda i, j: (i, j, 0))
    gspec = pl.BlockSpec((1, d), lambda i, j: (0, 0))
    mspec = pl.BlockSpec((1, 1, d), lambda i, j: (i, 0, 0))
    if scale is None:
        body, specs, args = _norm_body, [xspec, gspec], (x, gain.reshape(1, d))
    else:
        body, specs = _norm_mod_body, [xspec, gspec, mspec, mspec]
        args = (x, gain.reshape(1, d), scale, shift)
    return pl.pallas_call(
        body, grid=(b, s // ts), in_specs=specs, out_specs=xspec,
        out_shape=jax.ShapeDtypeStruct((b, s, d), out_dtype),
        compiler_params=_cparams(("parallel", "parallel")),
        name="rms_norm",
    )(*args)


def _mm_nt_body(x_ref, w_ref, o_ref):
    o_ref[...] = _dot_nt(x_ref[...], w_ref[...]).astype(o_ref.dtype)


def _wspec(w, layer, tn):
    return pl.BlockSpec((None, w.shape[1], tn), lambda i, j: (layer, 0, j))


def _matmul_nt(x, w, layer, out_dtype, name, tm_pref=1024, tn_pref=1024):
    m, k = x.shape
    n = w.shape[1]
    tm, tn = _tile(m, tm_pref), _tile(n, tn_pref)
    return pl.pallas_call(
        _mm_nt_body, grid=(m // tm, n // tn),
        in_specs=[pl.BlockSpec((tm, k), lambda i, j: (i, 0)),
                  pl.BlockSpec((None, tn, k), lambda i, j: (layer, j, 0))],
        out_specs=pl.BlockSpec((tm, tn), lambda i, j: (i, j)),
        out_shape=jax.ShapeDtypeStruct((m, n), out_dtype),
        compiler_params=_cparams(("parallel", "parallel"), V7X_VMEM_LIMIT),
        name=name,
    )(x, w)


def _mm_res_body(x_ref, w_ref, r_ref, g_ref, o_ref):
    o_ref[...] = r_ref[...] + g_ref[0] * _dot(x_ref[...], w_ref[...])


def _matmul_residual(x, w, layer, res, gate, seq, name, tm_pref=1024, tn_pref=512):
    m, k = x.shape
    n = w.shape[2]
    tm, tn = _tile(seq, tm_pref), _tile(n, tn_pref)
    per_batch = seq // tm
    return pl.pallas_call(
        _mm_res_body, grid=(m // tm, n // tn),
        in_specs=[pl.BlockSpec((tm, k), lambda i, j: (i, 0)),
                  _wspec(w, layer, tn),
                  pl.BlockSpec((tm, tn), lambda i, j: (i, j)),
                  pl.BlockSpec((1, 1, tn), lambda i, j: (i // per_batch, 0, j))],
        out_specs=pl.BlockSpec((tm, tn), lambda i, j: (i, j)),
        out_shape=jax.ShapeDtypeStruct((m, n), F32),
        compiler_params=_cparams(("parallel", "parallel"), V7X_VMEM_LIMIT),
        name=name,
    )(x, w, res, gate)


def _merge_body(ya_ref, yb_ref, yc_ref, wa_ref, wb_ref, wc_ref,
                ga_ref, gb_ref, gc_ref, bias_ref, o_ref):
    bias = bias_ref[...]
    acc = jax.nn.sigmoid(ga_ref[...] + bias[0]) * _dot(ya_ref[...], wa_ref[...])
    acc = acc + jax.nn.sigmoid(gb_ref[...] + bias[1]) * _dot(yb_ref[...], wb_ref[...])
    acc = acc + jax.nn.sigmoid(gc_ref[...] + bias[2]) * _dot(yc_ref[...], wc_ref[...])
    o_ref[...] = acc.astype(o_ref.dtype)


def _merge(ya, yb, yc, wa, wb, wc, layer, pg, bias, tm_pref=1024, tn_pref=512):
    m = ya.shape[0]
    d = wa.shape[2]
    tm, tn = _tile(m, tm_pref), _tile(d, tn_pref)
    nb = d // tn

    def yspec(y):
        return pl.BlockSpec((tm, y.shape[1]), lambda i, j: (i, 0))

    def gspec(k):
        return pl.BlockSpec((tm, tn), lambda i, j: (i, j + k * nb))

    return pl.pallas_call(
        _merge_body, grid=(m // tm, nb),
        in_specs=[yspec(ya), yspec(yb), yspec(yc),
                  _wspec(wa, layer, tn), _wspec(wb, layer, tn), _wspec(wc, layer, tn),
                  gspec(0), gspec(1), gspec(2),
                  pl.BlockSpec((None, 3, 1, tn), lambda i, j: (layer, 0, 0, j))],
        out_specs=pl.BlockSpec((tm, tn), lambda i, j: (i, j)),
        out_shape=jax.ShapeDtypeStruct((m, d), BF16),
        compiler_params=_cparams(("parallel", "parallel"), V7X_VMEM_LIMIT),
        name="merge",
    )(ya, yb, yc, wa, wb, wc, pg, pg, pg, bias)


def _head_ones(width):
    r = lax.broadcasted_iota(jnp.int32, (width, width), 0) // A_HEAD_DIM
    c = lax.broadcasted_iota(jnp.int32, (width, width), 1) // A_HEAD_DIM
    return (r == c).astype(BF16)


def _head_sum(x, ones):
    pieces = [_dot_exact_rhs(x[:, p:p + LANE], ones) for p in range(0, x.shape[1], LANE)]
    return pieces[0] if len(pieces) == 1 else jnp.concatenate(pieces, axis=1)


def _rwkv_prep_body(dims, rkv_ref, sm_ref, mu_rkv_ref, mu_sm_ref, w0_ref, w2_ref, a0_ref,
                    a2_ref, g2_ref, kks_ref, ka_ref,
                    r_ref, lw_ref, k_ref, v_ref, a_ref, b_ref, gate_ref,
                    prev_rkv, prev_sm):
    w, lora_pad = dims
    ts = rkv_ref.shape[1]

    @pl.when(pl.program_id(1) == 0)
    def _():
        prev_rkv[...] = jnp.zeros_like(prev_rkv)
        prev_sm[...] = jnp.zeros_like(prev_sm)

    def shifted(cur, prev_ref):
        rolled = pltpu.roll(cur, 1, axis=0)
        row = lax.broadcasted_iota(jnp.int32, cur.shape, 0)
        prev = jnp.where(row == 0, prev_ref[...], rolled)
        prev_ref[...] = cur[ts - 1:ts, :]
        return prev

    rkv = rkv_ref[0]
    sm = sm_ref[0]
    z = rkv + (shifted(rkv, prev_rkv) - rkv) * mu_rkv_ref[...]
    zs = sm + (shifted(sm, prev_sm) - sm) * mu_sm_ref[...]
    r, k, v = z[:, :w], z[:, w:2 * w], z[:, 2 * w:3 * w]
    zl = zs[:, :w2_ref.shape[0]]
    zg = zs[:, lora_pad - g2_ref.shape[0]:lora_pad]
    log_w = -jax.nn.softplus(-(w0_ref[...] + _dot_x(jnp.tanh(zl), w2_ref[...], 1))) - 0.5
    iclr = jax.nn.sigmoid(a0_ref[...] + _dot_x(zl, a2_ref[...], 1))
    gate = _dot_x(jax.nn.sigmoid(zg), g2_ref[...], 1)

    ones = _head_ones(LANE)
    kk = k * kks_ref[...]
    norm = jnp.sqrt(_head_sum(kk * kk, ones))
    kk = kk / jnp.maximum(norm, 1e-12)

    r_ref[0] = r
    lw_ref[0] = -jnp.exp(log_w)
    k_ref[0] = k * (1.0 + (iclr - 1.0) * ka_ref[...])
    v_ref[0] = v
    a_ref[0] = -kk
    b_ref[0] = kk * iclr
    gate_ref[0] = gate


def _rwkv_prep(rkv, small, mu_rkv, mu_small, w0, w2, a0, a2, g2, kk_scale, k_a):
    b, s, w3 = rkv.shape
    w = w3 // 3
    ns = small.shape[2]
    dl, il, gl = w2.shape[0], a2.shape[0], g2.shape[0]
    lora_pad = -(-(dl + il + gl) // LANE) * LANE
    k1 = -(-(dl + il) // LANE) * LANE
    g0 = (dl + il) // LANE * LANE
    w2 = jnp.pad(w2, ((0, k1 - dl), (0, 0)))
    a2 = jnp.pad(a2, ((dl, k1 - dl - il), (0, 0)))
    g2 = jnp.pad(g2, ((dl + il - g0, lora_pad - dl - il - gl), (0, 0)))
    ts = _tile(s, 256)
    row = lambda arr: arr.reshape(1, -1)
    full = lambda arr: pl.BlockSpec(arr.shape, lambda i, j: (0,) * arr.ndim)
    params = [row(mu_rkv), row(mu_small), row(w0), w2, row(a0), a2, g2, row(kk_scale), row(k_a)]
    out_spec = pl.BlockSpec((1, ts, w), lambda i, j: (i, j, 0))
    out = jax.ShapeDtypeStruct((b, s, w), F32)
    return pl.pallas_call(
        functools.partial(_rwkv_prep_body, (w, lora_pad)),
        grid=(b, s // ts),
        in_specs=[pl.BlockSpec((1, ts, w3), lambda i, j: (i, j, 0)),
                  pl.BlockSpec((1, ts, ns), lambda i, j: (i, j, 0))] + [full(p) for p in params],
        out_specs=[out_spec] * 7,
        out_shape=[out] * 7,
        scratch_shapes=[pltpu.VMEM((1, w3), F32), pltpu.VMEM((1, ns), F32)],
        compiler_params=_cparams(("parallel", "arbitrary"), V7X_VMEM_LIMIT),
        name="rwkv_prep",
    )(rkv, small, *params)


def _rwkv_rec_body(r_ref, lw_ref, k_ref, v_ref, a_ref, b_ref, gate_ref, rk_ref, gnw_ref,
                   gnb_ref, o_ref, state):
    c = RWKV_CHUNK
    n_pairs = r_ref.shape[2] // LANE

    @pl.when(pl.program_id(1) == 0)
    def _():
        state[...] = jnp.zeros_like(state)

    row = lax.broadcasted_iota(jnp.int32, (c, c), 0)
    col = lax.broadcasted_iota(jnp.int32, (c, c), 1)
    tri_incl = (row >= col).astype(BF16)

    r2 = lax.broadcasted_iota(jnp.int32, (2 * c, LANE), 0)
    c2 = lax.broadcasted_iota(jnp.int32, (2 * c, LANE), 1)
    same_head = (r2 // c) == (c2 // A_HEAD_DIM)
    t_row, t_col = r2 % c, c2 % c
    strict = same_head & (t_row > t_col)
    incl = same_head & (t_row >= t_col)
    eye = (r2 == c2).astype(F32)
    ones = _head_ones(LANE)

    def bd(x):
        return jnp.where(same_head, jnp.concatenate([x, x], axis=0), 0.0)

    cat0 = lambda *xs: jnp.concatenate(xs, axis=0)
    cat1 = lambda *xs: jnp.concatenate(xs, axis=1)
    hh = 2 * c
    pairs = range(n_pairs)
    sls = [slice(p * LANE, (p + 1) * LANE) for p in pairs]
    each = lambda fn, *cols: [fn(*args) for args in zip(*cols)]
    load = lambda ref: [ref[0, :, sl] for sl in sls]
    r, lw, k, v, a, b = (load(ref) for ref in (r_ref, lw_ref, k_ref, v_ref, a_ref, b_ref))

    cl = each(lambda x: _dot_exact_lhs(tri_incl, x), lw)
    cl_last = each(lambda x: x[c - 1:c, :], cl)
    e_neg = each(lambda x: jnp.exp(-x), cl)
    xa = each(lambda a_, cl_, lw_: bd(a_ * jnp.exp(cl_ - lw_)), a, cl, lw)
    xr = each(lambda r_, cl_: bd(r_ * jnp.exp(cl_)), r, cl)
    yb = each(lambda b_, e_: bd(b_ * e_), b, e_neg)
    yk = each(lambda k_, e_: bd(k_ * e_), k, e_neg)
    e_end = each(lambda last, cl_: jnp.exp(last - cl_), cl_last, cl)
    bh = each(lambda b_, e_: bd(b_ * e_), b, e_end)
    kh = each(lambda k_, e_: bd(k_ * e_), k, e_end)
    vv = each(bd, v)
    w_end = each(jnp.exp, cl_last)

    amat = each(lambda xa_, xr_, yb_, yk_: _dot_x(cat0(xa_, xr_), cat0(yb_, yk_),
                                                  RWKV_PASSES[0], _dot_nt), xa, xr, yb, yk)
    a_ab = each(lambda m_: jnp.where(strict, m_[:hh, :hh], 0.0), amat)
    a_ak = each(lambda m_: jnp.where(strict, m_[:hh, hh:], 0.0), amat)
    a_rb = each(lambda m_: jnp.where(incl, m_[hh:, :hh], 0.0), amat)
    a_rk = each(lambda m_: jnp.where(incl, m_[hh:, hh:], 0.0), amat)

    pt = RWKV_PASSES[1]
    t_inv = each(lambda x: eye + x, a_ab)
    pw = each(lambda x: _dot_x(x, x, pt), a_ab)
    n = 2
    while 2 * n < c:
        both = each(lambda pw_, t_: _dot_x(pw_, cat1(pw_, t_), pt), pw, t_inv)
        pw = each(lambda x: x[:, :hh], both)
        t_inv = each(lambda t_, x: t_ + x[:, hh:], t_inv, both)
        n *= 2
    t_inv = each(lambda pw_, t_: t_ + _dot_x(pw_, t_, pt), pw, t_inv)

    po = RWKV_PASSES[2]
    akv = each(lambda x, y: _dot_x(x, y, po), a_ak, vv)
    pq = each(lambda t_, xa_, akv_: _dot_x(t_, cat1(xa_, akv_), po), t_inv, xa, akv)
    rhs = each(lambda pq_, vv_: cat0(pq_, cat1(jnp.zeros_like(vv_), vv_)), pq, vv)
    ytop = each(lambda x, y, rhs_: _dot_x(cat1(x, y), rhs_, po), a_rb, a_rk, rhs)
    sbot = each(lambda x, y, rhs_: _dot_x(cat0(x, y), rhs_, po, _dot_tn), bh, kh, rhs)
    g = each(lambda xr_, t: xr_ + t[:, :hh], xr, ytop)
    m = each(lambda w, t: eye * w + t[:, :hh], w_end, sbot)

    gm = [_dot_x(cat0(g[p], m[p]), state[p], RWKV_PASSES[3]) for p in pairs]
    for p in pairs:
        state[p] = gm[p][hh:] + sbot[p][:, hh:]
    y_bd = each(lambda gm_, t: gm_[:hh] + t[:, hh:], gm, ytop)
    y = each(lambda x: x[:c] + x[c:], y_bd)

    mean = each(lambda x: _head_sum(x, ones) * (1.0 / A_HEAD_DIM), y)
    yc = each(lambda x, mu: x - mu, y, mean)
    var = each(lambda x: _head_sum(x * x, ones) * (1.0 / A_HEAD_DIM), yc)
    bonus = [_head_sum(r[p] * k[p] * rk_ref[:, sls[p]], ones) * v[p] for p in pairs]
    outs = [(yc[p] * lax.rsqrt(var[p] + A_GN_EPS) * gnw_ref[:, sls[p]] + gnb_ref[:, sls[p]]
             + bonus[p]) * gate_ref[0, :, sls[p]] for p in pairs]
    o_ref[0] = jnp.concatenate(outs, axis=1).astype(o_ref.dtype)


def _rwkv_recurrence(r, lw, k, v, a, b, gate, r_k, gn_w, gn_b):
    bsz, s, w = r.shape
    c = RWKV_CHUNK
    blk = pl.BlockSpec((1, c, w), lambda i, j: (i, j, 0))
    par = pl.BlockSpec((1, w), lambda i, j: (0, 0))
    return pl.pallas_call(
        _rwkv_rec_body, grid=(bsz, s // c),
        in_specs=[blk] * 7 + [par] * 3,
        out_specs=blk,
        out_shape=jax.ShapeDtypeStruct((bsz, s, w), BF16),
        scratch_shapes=[pltpu.VMEM((w // LANE, LANE, LANE), F32)],
        compiler_params=_cparams(("parallel", "arbitrary")),
        name="rwkv_recurrence",
    )(r, lw, k, v, a, b, gate, r_k.reshape(1, w), gn_w.reshape(1, w), gn_b.reshape(1, w))


def _fox_cum_body(f_ref, bias_ref, cum_ref, cum_t_ref):
    s = f_ref.shape[1]
    t = _tile(s, 256)
    row = lax.broadcasted_iota(jnp.int32, (t, t), 0)
    col = lax.broadcasted_iota(jnp.int32, (t, t), 1)
    tri = (row >= col).astype(BF16)
    carry = jnp.zeros((1, LANE), F32)
    for i in range(s // t):
        ls = jax.nn.log_sigmoid(f_ref[0, i * t:(i + 1) * t, :] + bias_ref[...])
        cum = _dot_exact_lhs(tri, ls) + carry
        carry = cum[t - 1:t, :]
        cum_ref[0, i * t:(i + 1) * t, :] = cum
        cum_t_ref[0, :, i * t:(i + 1) * t] = cum.T


def _fox_cum(small, lane_block, bias):
    b, s, _ = small.shape
    return pl.pallas_call(
        _fox_cum_body, grid=(b,),
        in_specs=[pl.BlockSpec((1, s, LANE), lambda i: (i, 0, lane_block)),
                  pl.BlockSpec((1, LANE), lambda i: (0, 0))],
        out_specs=[pl.BlockSpec((1, s, LANE), lambda i: (i, 0, 0)),
                   pl.BlockSpec((1, LANE, s), lambda i: (i, 0, 0))],
        out_shape=[jax.ShapeDtypeStruct((b, s, LANE), F32),
                   jax.ShapeDtypeStruct((b, LANE, s), F32)],
        compiler_params=_cparams(("parallel",)),
        name="fox_cum",
    )(small, bias)


def _fox_body(tq, q_ref, k_ref, v_ref, cum_ref, cum_t_ref, o_ref):
    s = q_ref.shape[1]
    dh = ATTN_HEAD_DIM
    group = q_ref.shape[2] // dh
    heads = [slice(i * dh, (i + 1) * dh) for i in range(group)]
    each = lambda fn, *cols: [fn(*args) for args in zip(*cols)]
    scale = dh ** -0.5
    lane_row = lax.broadcasted_iota(jnp.int32, (LANE, LANE), 0)
    sel = [(lane_row == pl.program_id(1) * group + i).astype(BF16) for i in range(group)]
    row = lax.broadcasted_iota(jnp.int32, (tq, tq), 0)
    col = lax.broadcasted_iota(jnp.int32, (tq, tq), 1)
    causal = row >= col
    rowmax = lambda x: jnp.max(x, axis=1, keepdims=True)
    rowsum = lambda x: jnp.sum(x, axis=1, keepdims=True)

    for qi in range(s // tq):
        q0, q1 = qi * tq, (qi + 1) * tq
        q = [q_ref[0, q0:q1, h] for h in heads]
        cum_q = [_dot_exact_rhs(cum_ref[0, q0:q1, :], sel_) for sel_ in sel]

        def logits(k0, k1):
            return [_dot_nt(q[i], k_ref[0, k0:k1, heads[i]]) * scale
                    + jnp.concatenate([cum_q[i]] * ((k1 - k0) // LANE), axis=1)
                    - cum_t_ref[0, i, :, k0:k1] for i in range(group)]

        s_diag = each(lambda x: jnp.where(causal, x, -jnp.inf), logits(q0, q1))
        m = each(rowmax, s_diag)
        if qi:
            s_past = logits(0, q0)
            m = each(lambda m_, x: jnp.maximum(m_, rowmax(x)), m, s_past)
        p = each(lambda x, m_: jnp.exp(x - m_), s_diag, m)
        l = each(rowsum, p)
        acc = [_dot(p_.astype(BF16), v_ref[0, q0:q1, h]) for p_, h in zip(p, heads)]
        if qi:
            p = each(lambda x, m_: jnp.exp(x - m_), s_past, m)
            l = each(lambda l_, p_: l_ + rowsum(p_), l, p)
            acc = [a_ + _dot(p_.astype(BF16), v_ref[0, :q0, h])
                   for a_, p_, h in zip(acc, p, heads)]
        for a_, l_, h in zip(acc, l, heads):
            o_ref[0, q0:q1, h] = (a_ / l_).astype(o_ref.dtype)


def _fox_attention(qkv, cum, cum_t, n_heads):
    b, s, _ = qkv.shape
    tq = _tile(s, ATTN_Q_TILE)
    group = FOX_HEAD_GROUP if n_heads % FOX_HEAD_GROUP == 0 else 1
    ng = n_heads // group
    blk = lambda off: pl.BlockSpec((1, s, group * ATTN_HEAD_DIM), lambda i, h: (i, 0, off + h))
    return pl.pallas_call(
        functools.partial(_fox_body, tq), grid=(b, ng),
        in_specs=[blk(0), blk(ng), blk(2 * ng),
                  pl.BlockSpec((1, s, LANE), lambda i, h: (i, 0, 0)),
                  pl.BlockSpec((1, group, 1, s), lambda i, h: (i, h, 0, 0))],
        out_specs=blk(0),
        out_shape=jax.ShapeDtypeStruct((b, s, n_heads * ATTN_HEAD_DIM), BF16),
        compiler_params=_cparams(("parallel", "parallel"), V7X_VMEM_LIMIT),
        name="fox_attention",
    )(qkv, qkv, qkv, cum, cum_t.reshape(b, LANE, 1, s))


def _log_sigmoid(z):
    return jnp.minimum(z, 0.0) - jnp.log(1.0 + jnp.exp(-jnp.abs(z)))


def _sb_body(t, q_ref, k_ref, v_ref, o_ref):
    s = q_ref.shape[1]
    dh = ATTN_HEAD_DIM
    heads = [slice(i * dh, (i + 1) * dh) for i in range(q_ref.shape[2] // dh)]
    each = lambda fn, *cols: [fn(*args) for args in zip(*cols)]
    scale = dh ** -0.5
    row = lax.broadcasted_iota(jnp.int32, (t, t), 0)
    col = lax.broadcasted_iota(jnp.int32, (t, t), 1)
    strict = row > col
    after = strict.astype(BF16)

    for qi in range(s // t):
        rows = slice(qi * t, (qi + 1) * t)
        q = [q_ref[0, rows, h] for h in heads]
        later_rows = [jnp.zeros((t, 1), F32) for _ in heads]
        acc = [jnp.zeros((t, dh), F32) for _ in heads]
        for kj in range(qi, -1, -1):
            keys = slice(kj * t, (kj + 1) * t)
            z = [_dot_nt(q_, k_ref[0, keys, h]) * scale for q_, h in zip(q, heads)]
            log_take = each(_log_sigmoid, z)
            log_stay = each(lambda lt, z_: lt - z_, log_take, z)
            if kj == qi:
                log_stay = each(lambda x: jnp.where(strict, x, 0.0), log_stay)
            later = each(lambda lr, ls: lr + _dot_exact_rhs(ls, after, 2), later_rows, log_stay)
            wts = each(lambda lt, la: jnp.exp(lt + la), log_take, later)
            if kj == qi:
                wts = each(lambda x: jnp.where(strict, x, 0.0), wts)
            acc = [a_ + _dot(w_.astype(BF16), v_ref[0, keys, h])
                   for a_, w_, h in zip(acc, wts, heads)]
            if kj:
                later_rows = each(lambda lr, ls: lr + jnp.sum(ls, axis=1, keepdims=True),
                                  later_rows, log_stay)
        for a_, h in zip(acc, heads):
            o_ref[0, rows, h] = a_.astype(o_ref.dtype)


def _sb_attention(qkv, n_heads):
    b, s, _ = qkv.shape
    t = _tile(s, ATTN_Q_TILE)
    group = SB_HEAD_GROUP if n_heads % SB_HEAD_GROUP == 0 else 1
    ng = n_heads // group
    blk = lambda off: pl.BlockSpec((1, s, group * ATTN_HEAD_DIM), lambda i, h: (i, 0, off + h))
    return pl.pallas_call(
        functools.partial(_sb_body, t), grid=(b, ng),
        in_specs=[blk(0), blk(ng), blk(2 * ng)],
        out_specs=blk(0),
        out_shape=jax.ShapeDtypeStruct((b, s, n_heads * ATTN_HEAD_DIM), BF16),
        compiler_params=_cparams(("parallel", "parallel"), V7X_VMEM_LIMIT),
        name="sb_attention",
    )(qkv, qkv, qkv)


def _ffn_up_body(per_seq, rows, x_ref, wu_ref, wv_ref, cw_ref, cb_ref, o_ref, tail):
    tm = x_ref.shape[0]

    @pl.when(pl.program_id(1) % per_seq == 0)
    def _():
        tail[...] = jnp.zeros_like(tail)

    cw = cw_ref[...]
    cb = cb_ref[...]
    wu, wv = wu_ref[...], wv_ref[...]
    row = lax.broadcasted_iota(jnp.int32, (rows, wu.shape[1]), 0)
    project = lambda r: (_dot(x_ref[r * rows:(r + 1) * rows, :], wu),
                         _dot(x_ref[r * rows:(r + 1) * rows, :], wv))
    prev = tail[...]
    u, v = project(0)
    for r in range(tm // rows):
        nxt = project(r + 1) if (r + 1) * rows < tm else None
        u1 = jnp.where(row == 0, prev[1:2], pltpu.roll(u, 1, axis=0))
        u2 = jnp.where(row == 0, prev[0:1],
                       jnp.where(row == 1, prev[1:2], pltpu.roll(u, 2, axis=0)))
        conv = u2 * cw[0:1] + u1 * cw[1:2] + u * cw[2:3] + cb
        act = 0.5 * conv * (1.0 + lax.erf(conv * (2.0 ** -0.5)))
        o_ref[r * rows:(r + 1) * rows, :] = (act * v).astype(o_ref.dtype)
        prev = u[rows - 2:rows, :]
        if nxt is not None:
            u, v = nxt
    tail[...] = prev


def _ffn_up_glu(x, w_up, layer, conv_w, conv_b, seq, tm_pref=1024, tn_pref=512):
    m, k = x.shape
    f = w_up.shape[2] // 2
    tm, tn = _tile(seq, tm_pref), _tile(f, tn_pref)
    nf = f // tn
    rows = _tile(tm, 256)
    wspec = lambda off: pl.BlockSpec((None, k, tn), lambda j, i: (layer, 0, j + off))
    return pl.pallas_call(
        functools.partial(_ffn_up_body, seq // tm, rows), grid=(nf, m // tm),
        in_specs=[pl.BlockSpec((tm, k), lambda j, i: (i, 0)), wspec(0), wspec(nf),
                  pl.BlockSpec((conv_w.shape[0], tn), lambda j, i: (0, j)),
                  pl.BlockSpec((1, tn), lambda j, i: (0, j))],
        out_specs=pl.BlockSpec((tm, tn), lambda j, i: (i, j)),
        out_shape=jax.ShapeDtypeStruct((m, f), BF16),
        scratch_shapes=[pltpu.VMEM((2, tn), F32)],
        compiler_params=_cparams(("parallel", "arbitrary"), V7X_VMEM_LIMIT),
        name="ffn_up_glu",
    )(x, w_up, w_up, conv_w, conv_b.reshape(1, f))


def kernel(x, c, w_ada, b_ada, ada_table, norm_mix, w_in, a_mu, a_w0, a_w2, a_a0, a_a2, a_g2,
           a_kk_scale, a_ka, a_rk, a_gn_w, a_gn_b, b_fbias, merge_bias, w_br_a, w_br_b, w_br_c,
           w_out, norm_ffn, w_up, conv_w, conv_b, w_down, norm_final):
    bsz, seq, d = x.shape
    depth = ada_table.shape[0]
    m = bsz * seq
    aw = a_w0.shape[1]
    lora = a_w2.shape[1] + a_a2.shape[1] + a_g2.shape[1]
    bw, cw = w_br_b.shape[1], w_br_c.shape[1]
    b_heads, c_heads = bw // ATTN_HEAD_DIM, cw // ATTN_HEAD_DIM
    a_cols = 3 * aw + lora
    b_cols = 3 * bw + b_heads
    lora_pad = -(-lora // LANE) * LANE

    c_start = a_cols + b_cols
    g_start = c_start + 3 * cw
    w_in_t = jnp.transpose(w_in, (2, 0, 1))

    def cols(a, b, pad_to=None):
        g = jnp.transpose(w_in_t[a:b], (1, 0, 2)).astype(BF16)
        return g if pad_to is None else jnp.pad(g, ((0, 0), (0, pad_to - (b - a)), (0, 0)))

    w_rkv = cols(0, 3 * aw)
    w_small = jnp.concatenate([cols(3 * aw, a_cols, lora_pad),
                               cols(a_cols + 3 * bw, c_start, LANE)], axis=1)
    w_bqkv = cols(a_cols, a_cols + 3 * bw)
    w_cqkv = cols(c_start, g_start)
    w_gate = cols(g_start, w_in.shape[2])
    w_br_a, w_br_b, w_br_c = w_br_a.astype(BF16), w_br_b.astype(BF16), w_br_c.astype(BF16)
    w_out, w_up, w_down = w_out.astype(BF16), w_up.astype(BF16), w_down.astype(BF16)
    mu_rkv = a_mu[:, :3 * aw]
    mu_small = jnp.pad(a_mu[:, 3 * aw:], ((0, 0), (0, lora_pad + LANE - lora)))
    fbias = jnp.pad(b_fbias, ((0, 0), (0, LANE - b_heads)))

    mods = _adaln(c, w_ada, b_ada, ada_table).reshape(depth, bsz, -1, 1, d)

    for l in range(depth):
        shift_m, scale_m, gate_m = mods[l, :, 0], mods[l, :, 1], mods[l, :, 2]
        shift_f, scale_f, gate_f = mods[l, :, 3], mods[l, :, 4], mods[l, :, 5]

        h = _rms_norm(x, norm_mix[l], scale_m, shift_m).reshape(m, d)
        rkv = _matmul_nt(h, w_rkv, l, F32, "proj_rkv").reshape(bsz, seq, -1)
        small = _matmul_nt(h, w_small, l, F32, "proj_small").reshape(bsz, seq, -1)
        bqkv = _matmul_nt(h, w_bqkv, l, BF16, "proj_fox", tn_pref=768).reshape(bsz, seq, -1)
        cqkv = _matmul_nt(h, w_cqkv, l, BF16, "proj_sb", tn_pref=768).reshape(bsz, seq, -1)
        pg = _matmul_nt(h, w_gate, l, F32, "proj_gate")

        r, lw, k, v, a, b, gate = _rwkv_prep(
            rkv, small, mu_rkv[l], mu_small[l], a_w0[l], a_w2[l], a_a0[l], a_a2[l], a_g2[l],
            a_kk_scale[l].reshape(-1), a_ka[l].reshape(-1))
        y_a = _rwkv_recurrence(r, lw, k, v, a, b, gate, a_rk[l].reshape(-1), a_gn_w[l], a_gn_b[l])

        cum, cum_t = _fox_cum(small, lora_pad // LANE, fbias[l].reshape(1, LANE))
        y_b = _fox_attention(bqkv, cum, cum_t, b_heads)
        y_c = _sb_attention(cqkv, c_heads)

        merged = _merge(y_a.reshape(m, aw), y_b.reshape(m, bw), y_c.reshape(m, cw),
                        w_br_a, w_br_b, w_br_c, l, pg, merge_bias.reshape(depth, 3, 1, d))
        x = _matmul_residual(merged, w_out, l, x.reshape(m, d), gate_m, seq,
                             "out_proj").reshape(bsz, seq, d)

        h = _rms_norm(x, norm_ffn[l], scale_f, shift_f).reshape(m, d)
        act = _ffn_up_glu(h, w_up, l, conv_w[l], conv_b[l], seq)
        x = _matmul_residual(act, w_down, l, x.reshape(m, d), gate_f, seq,
                             "ffn_down").reshape(bsz, seq, d)

    return _rms_norm(x, norm_final, out_dtype=x.dtype)
```
